```python
import math
import jax, jax.numpy as jnp
from jax import lax
import numpy as np

D_MODEL = 1024
BATCH = 16
SEQ = 2048
DEPTH = 2
DEC_BATCH = 128
DEC_SEQ = 1
PAST_LEN = 16384
PAGE_SIZE = 128

D_MIX = D_MODEL
SSM_GROUPS = 16
SSM_CH = 16
SSM_WIDTH = SSM_GROUPS * SSM_CH
SSM_STATE = 64
SSM_DT_MIN = 0.001
SSM_DT_MAX = 0.1
MLA_HEADS = 8
MLA_NOPE = 64
MLA_ROPE = 32
MLA_V = 64
MLA_Q_RANK = 384
MLA_KV_RANK = 256
MLA_WIDTH = MLA_HEADS * MLA_V
ROPE_THETA = 10000.0
Q_BLOCK = 128
MLSTM_HEADS = 4
MLSTM_DH = 64
MLSTM_WIDTH = MLSTM_HEADS * MLSTM_DH
MLSTM_CHUNK = 64
D_FF = -(-8 * D_MODEL // (3 * 256)) * 256
RMS_EPS = 1e-6
IN_SPLITS = (SSM_WIDTH, MLA_Q_RANK, MLA_KV_RANK, MLA_ROPE,
             MLSTM_WIDTH, MLSTM_WIDTH, MLSTM_WIDTH, MLSTM_WIDTH, MLSTM_HEADS, MLSTM_HEADS)
N_IN = SSM_WIDTH + MLA_Q_RANK + MLA_KV_RANK + MLA_ROPE + 4 * MLSTM_WIDTH + 2 * MLSTM_HEADS

kernel_name = 'hymba_s5_mla_mlstm_sandwich_decoder_step'


def rmsnorm(x, g):
    xf = x.astype(jnp.float32)
    y = xf * lax.rsqrt(jnp.mean(xf * xf, axis=-1, keepdims=True) + RMS_EPS)
    return (y * g.astype(jnp.float32)).astype(x.dtype)


def rope(x, pos):
    half = MLA_ROPE // 2
    inv = ROPE_THETA ** (-jnp.arange(half, dtype=jnp.float32) / half)
    ang = pos.astype(jnp.float32)[:, None] * inv[None, :]
    cos = jnp.cos(ang)[:, None, :]
    sin = jnp.sin(ang)[:, None, :]
    xf = x.astype(jnp.float32)
    x1, x2 = xf[..., :half], xf[..., half:]
    return jnp.concatenate([x1 * cos - x2 * sin, x1 * sin + x2 * cos], axis=-1)


def _complex_affine_combine(e1, e2):
    a1r, a1i, b1r, b1i = e1
    a2r, a2i, b2r, b2i = e2
    return (a2r * a1r - a2i * a1i, a2r * a1i + a2i * a1r,
            a2r * b1r - a2i * b1i + b2r, a2r * b1i + a2i * b1r + b2i)


def ssm_scan(u, h0_re, h0_im, p):
    f32 = jnp.float32
    B_, T, _ = u.shape
    ug = u.reshape(B_, T, SSM_GROUPS, SSM_CH)
    a_re = p['ssm_a_re'].astype(f32)
    a_im = p['ssm_a_im'].astype(f32)
    dt = jnp.exp(p['ssm_log_dt'].astype(f32))[:, None]
    mag = jnp.exp(a_re * dt)
    lb_re = mag * jnp.cos(a_im * dt)
    lb_im = mag * jnp.sin(a_im * dt)
    inv = 1.0 / (a_re * a_re + a_im * a_im)
    f_re = ((lb_re - 1.0) * a_re + lb_im * a_im) * inv
    f_im = (lb_im * a_re - (lb_re - 1.0) * a_im) * inv
    b_re = p['ssm_b_re'].astype(f32)
    b_im = p['ssm_b_im'].astype(f32)
    bb_re = f_re[..., None] * b_re - f_im[..., None] * b_im
    bb_im = f_re[..., None] * b_im + f_im[..., None] * b_re
    bu_re = jnp.einsum('btgc,gpc->btgp', ug, bb_re)
    bu_im = jnp.einsum('btgc,gpc->btgp', ug, bb_im)
    shape = bu_re.shape
    elems = (jnp.broadcast_to(lb_re, shape), jnp.broadcast_to(lb_im, shape), bu_re, bu_im)
    ar_c, ai_c, hr, hi = lax.associative_scan(_complex_affine_combine, elems, axis=1)
    h0r = h0_re.astype(f32)[:, None]
    h0i = h0_im.astype(f32)[:, None]
    hr = hr + ar_c * h0r - ai_c * h0i
    hi = hi + ar_c * h0i + ai_c * h0r
    y = (jnp.einsum('btgp,gcp->btgc', hr, p['ssm_c_re'].astype(f32))
         - jnp.einsum('btgp,gcp->btgc', hi, p['ssm_c_im'].astype(f32)))
    y = y.reshape(B_, T, SSM_WIDTH) + p['ssm_d'].astype(f32) * u
    return y, hr[:, -1], hi[:, -1]


def mla_prompt(q_lat, q_rope, ckv, krope):
    B_, T, H, R = q_lat.shape
    nb = T // Q_BLOCK
    scale = (MLA_NOPE + MLA_ROPE) ** -0.5
    qlb = q_lat.reshape(B_, nb, Q_BLOCK, H, R).swapaxes(0, 1)
    qrb = q_rope.reshape(B_, nb, Q_BLOCK, H, MLA_ROPE).swapaxes(0, 1)
    kpos = jnp.arange(T)

    def block(args):
        i, ql, qr = args
        s = (jnp.einsum('bqhr,bkr->bhqk', ql, ckv) + jnp.einsum('bqhe,bke->bhqk', qr, krope)) * scale
        qpos = i * Q_BLOCK + jnp.arange(Q_BLOCK)
        s = jnp.where(kpos[None, :] <= qpos[:, None], s, -jnp.inf)
        pr = jax.nn.softmax(s, axis=-1)
        return jnp.einsum('bhqk,bkr->bqhr', pr, ckv)

    o = lax.map(block, (jnp.arange(nb), qlb, qrb))
    return o.swapaxes(0, 1).reshape(B_, T, H, R)


def mla_decode(q_lat, q_rope, ckv_new, kr_new, ckv_past, kr_past):
    T = q_lat.shape[1]
    n_past = ckv_past.shape[1]
    scale = (MLA_NOPE + MLA_ROPE) ** -0.5
    s_past = (jnp.einsum('bthr,bsr->bhts', q_lat, ckv_past)
              + jnp.einsum('bthe,bse->bhts', q_rope, kr_past)) * scale
    s_new = (jnp.einsum('bthr,bsr->bhts', q_lat, ckv_new)
             + jnp.einsum('bthe,bse->bhts', q_rope, kr_new)) * scale
    causal = jnp.tril(jnp.ones((T, T), dtype=bool))
    s_new = jnp.where(causal, s_new, -jnp.inf)
    pr = jax.nn.softmax(jnp.concatenate([s_past, s_new], axis=-1), axis=-1)
    return (jnp.einsum('bhts,bsr->bthr', pr[..., :n_past], ckv_past)
            + jnp.einsum('bhts,bsr->bthr', pr[..., n_past:], ckv_new))


def mlstm_chunkwise(q, k, v, ig, lf, C0, n0, m0, chunk):
    f32 = jnp.float32
    B_, H, T, D = q.shape
    nc = T // chunk
    split = lambda a: jnp.moveaxis(a.reshape(B_, H, nc, chunk, *a.shape[3:]), 2, 0)
    causal = jnp.tril(jnp.ones((chunk, chunk), dtype=bool))

    def step(carry, xs):
        C, n, m = carry
        qc, kc, vc, ic, fc = xs
        b = jnp.cumsum(fc, axis=-1)
        lw = jnp.where(causal, b[..., :, None] - b[..., None, :] + ic[..., None, :], -jnp.inf)
        m_inter = b + m[..., None]
        m_t = jnp.maximum(m_inter, jnp.max(lw, axis=-1))
        w = jnp.exp(lw - m_t[..., None])
        g = jnp.exp(m_inter - m_t)
        qk = jnp.einsum('bhtd,bhsd->bhts', qc, kc) * w
        num = g[..., None] * jnp.einsum('bhtd,bhde->bhte', qc, C) + jnp.einsum('bhts,bhse->bhte', qk, vc)
        den = g * jnp.einsum('bhtd,bhd->bht', qc, n) + jnp.sum(qk, axis=-1)
        h = num / jnp.maximum(jnp.abs(den), jnp.exp(-m_t))[..., None]
        w_end = w[..., -1, :]
        g_end = g[..., -1]
        C_new = g_end[..., None, None] * C + jnp.einsum('bhs,bhsd,bhse->bhde', w_end, kc, vc)
        n_new = g_end[..., None] * n + jnp.einsum('bhs,bhsd->bhd', w_end, kc)
        return (C_new, n_new, m_t[..., -1]), h

    carry0 = (C0.astype(f32), n0.astype(f32), m0.astype(f32))
    (C, n, m), hs = lax.scan(step, carry0, (split(q), split(k), split(v), split(ig), split(lf)))
    h = jnp.moveaxis(hs, 0, 2).reshape(B_, H, T, D)
    return h, (C, n, m)


def token_mixers(hn, pos, ssm_h0, mlstm_s0, mla_past, p):
    f32 = jnp.float32
    B_, T, _ = hn.shape
    z = (hn @ p['w_in']).astype(f32)
    idx = np.cumsum(IN_SPLITS)[:-1].tolist()
    u, cq, ckv, kr, mq, mk, mv, mo, mi, mf = jnp.split(z, idx, axis=-1)

    y_a, s_re, s_im = ssm_scan(u, ssm_h0[0], ssm_h0[1], p)
    y_a = jax.nn.gelu(y_a)
    y_a = y_a * jax.nn.sigmoid(y_a @ p['ssm_w_glu'].astype(f32) + p['ssm_b_glu'].astype(f32))

    cq = rmsnorm(cq, p['mla_q_norm'])
    q = (cq @ p['mla_w_uq'].astype(f32)).reshape(B_, T, MLA_HEADS, MLA_NOPE + MLA_ROPE)
    q_nope = q[..., :MLA_NOPE]
    q_rope = rope(q[..., MLA_NOPE:], pos)
    ckv = rmsnorm(ckv, p['mla_kv_norm'])
    kr = rope(kr[:, :, None, :], pos)[:, :, 0, :]
    w_uk = p['mla_w_uk'].astype(f32)
    q_lat = jnp.einsum('bthd,rhd->bthr', q_nope, w_uk)
    if mla_past is None:
        o_lat = mla_prompt(q_lat, q_rope, ckv, kr)
    else:
        o_lat = mla_decode(q_lat, q_rope, ckv, kr, mla_past[0], mla_past[1])
    y_b = jnp.einsum('bthr,rhd->bthd', o_lat, p['mla_w_uv'].astype(f32)).reshape(B_, T, MLA_WIDTH)

    heads = lambda a: a.reshape(B_, T, MLSTM_HEADS, MLSTM_DH).transpose(0, 2, 1, 3)
    ig = (mi + p['mlstm_b_i'].astype(f32)).transpose(0, 2, 1)
    lf = jax.nn.log_sigmoid(mf + p['mlstm_b_f'].astype(f32)).transpose(0, 2, 1)
    chunk = MLSTM_CHUNK if T % MLSTM_CHUNK == 0 else T
    h, (C, n, m) = mlstm_chunkwise(heads(mq), heads(mk) * (MLSTM_DH ** -0.5), heads(mv), ig, lf,
                                   mlstm_s0[0], mlstm_s0[1], mlstm_s0[2], chunk)
    y_c = h.transpose(0, 2, 1, 3).reshape(B_, T, MLSTM_WIDTH) * jax.nn.sigmoid(mo)

    y = jnp.concatenate([rmsnorm(y_a, p['out_norm_ssm']), rmsnorm(y_b, p['out_norm_mla']),
                         rmsnorm(y_c, p['out_norm_mlstm'])], axis=-1)
    out = (y @ p['w_out'].astype(f32)).astype(hn.dtype)
    return out, (ckv, kr, s_re, s_im, C, n, m)


def swiglu(x, wg, wu, wd):
    return (jax.nn.silu(x @ wg) * (x @ wu)) @ wd


def decoder_layer(x, pos, ssm_h0, mlstm_s0, mla_past, p):
    h = rmsnorm(x, p['norm_mix_pre'])
    mix, st = token_mixers(h, pos, ssm_h0, mlstm_s0, mla_past, p)
    x = x + rmsnorm(mix, p['norm_mix_post'])
    h = rmsnorm(x, p['norm_ffn_pre'])
    x = x + rmsnorm(swiglu(h, p['ffn_w_gate'], p['ffn_w_up'], p['ffn_w_down']), p['norm_ffn_post'])
    return x, st


def setup_inputs(seed: int = 0) -> dict:
    key = jax.random.key(seed)
    ks = iter(jax.random.split(key, 64))
    f32 = jnp.float32

    def nrm(shape, scale):
        return scale * jax.random.normal(next(ks), shape, f32)

    n_pages = PAST_LEN // PAGE_SIZE
    n_used = DEC_BATCH * n_pages
    n_pool = n_used + (n_used + 3) // 4
    perm = jax.random.permutation(next(ks), n_pool)
    page_table = perm[:n_used].reshape(DEC_BATCH, n_pages).astype(jnp.int32)
    gain = lambda d: 1.0 + nrm((DEPTH, d), 0.02)
    log_dt = math.log(SSM_DT_MIN) + jax.random.uniform(next(ks), (DEPTH, SSM_GROUPS), f32) * (
        math.log(SSM_DT_MAX) - math.log(SSM_DT_MIN))
    return {
        'x_prompt': nrm((BATCH, SEQ, D_MODEL), 1.0),
        'x_sample': nrm((DEC_BATCH, DEC_SEQ, D_MODEL), 1.0),
        'cache_ckv': nrm((DEPTH, n_pool, PAGE_SIZE, MLA_KV_RANK), 1.0),
        'cache_krope': nrm((DEPTH, n_pool, PAGE_SIZE, MLA_ROPE), 1.0),
        'page_table': page_table,
        'state_ssm_re': nrm((DEPTH, DEC_BATCH, SSM_GROUPS, SSM_STATE), 0.1),
        'state_ssm_im': nrm((DEPTH, DEC_BATCH, SSM_GROUPS, SSM_STATE), 0.1),
        'state_mlstm_C': nrm((DEPTH, DEC_BATCH, MLSTM_HEADS, MLSTM_DH, MLSTM_DH), 0.1),
        'state_mlstm_n': nrm((DEPTH, DEC_BATCH, MLSTM_HEADS, MLSTM_DH), 0.1),
        'state_mlstm_m': nrm((DEPTH, DEC_BATCH, MLSTM_HEADS), 0.5),
        'norm_mix_pre': gain(D_MODEL),
        'norm_mix_post': gain(D_MODEL),
        'norm_ffn_pre': gain(D_MODEL),
        'norm_ffn_post': gain(D_MODEL),
        'w_in': nrm((DEPTH, D_MODEL, N_IN), D_MODEL ** -0.5),
        'ssm_a_re': -0.5 + nrm((DEPTH, SSM_GROUPS, SSM_STATE), 0.01),
        'ssm_a_im': jnp.pi * jnp.arange(SSM_STATE, dtype=f32) + nrm((DEPTH, SSM_GROUPS, SSM_STATE), 0.01),
        'ssm_b_re': nrm((DEPTH, SSM_GROUPS, SSM_STATE, SSM_CH), (2 * SSM_CH) ** -0.5),
        'ssm_b_im': nrm((DEPTH, SSM_GROUPS, SSM_STATE, SSM_CH), (2 * SSM_CH) ** -0.5),
        'ssm_c_re': nrm((DEPTH, SSM_GROUPS, SSM_CH, SSM_STATE), (2 * SSM_STATE) ** -0.5),
        'ssm_c_im': nrm((DEPTH, SSM_GROUPS, SSM_CH, SSM_STATE), (2 * SSM_STATE) ** -0.5),
        'ssm_d': nrm((DEPTH, SSM_WIDTH), 1.0),
        'ssm_log_dt': log_dt,
        'ssm_w_glu': nrm((DEPTH, SSM_WIDTH, SSM_WIDTH), SSM_WIDTH ** -0.5),
        'ssm_b_glu': nrm((DEPTH, SSM_WIDTH), 0.01),
        'mla_q_norm': gain(MLA_Q_RANK),
        'mla_w_uq': nrm((DEPTH, MLA_Q_RANK, MLA_HEADS * (MLA_NOPE + MLA_ROPE)), MLA_Q_RANK ** -0.5),
        'mla_kv_norm': gain(MLA_KV_RANK),
        'mla_w_uk': nrm((DEPTH, MLA_KV_RANK, MLA_HEADS, MLA_NOPE), MLA_KV_RANK ** -0.5),
        'mla_w_uv': nrm((DEPTH, MLA_KV_RANK, MLA_HEADS, MLA_V), MLA_KV_RANK ** -0.5),
        'mlstm_b_i': nrm((DEPTH, MLSTM_HEADS), 0.1),
        'mlstm_b_f': jnp.linspace(3.0, 6.0, MLSTM_HEADS, dtype=f32)[None, :] + nrm((DEPTH, MLSTM_HEADS), 0.01),
        'out_norm_ssm': gain(SSM_WIDTH),
        'out_norm_mla': gain(MLA_WIDTH),
        'out_norm_mlstm': gain(MLSTM_WIDTH),
        'w_out': nrm((DEPTH, D_MIX, D_MODEL), D_MIX ** -0.5),
        'ffn_w_gate': nrm((DEPTH, D_MODEL, D_FF), D_MODEL ** -0.5),
        'ffn_w_up': nrm((DEPTH, D_MODEL, D_FF), D_MODEL ** -0.5),
        'ffn_w_down': nrm((DEPTH, D_FF, D_MODEL), D_FF ** -0.5),
    }


def reference(x_prompt, x_sample, cache_ckv, cache_krope, page_table, state_ssm_re, state_ssm_im,
              state_mlstm_C, state_mlstm_n, state_mlstm_m, norm_mix_pre, norm_mix_post, norm_ffn_pre,
              norm_ffn_post, w_in, ssm_a_re, ssm_a_im, ssm_b_re, ssm_b_im, ssm_c_re, ssm_c_im, ssm_d,
              ssm_log_dt, ssm_w_glu, ssm_b_glu, mla_q_norm, mla_w_uq, mla_kv_norm, mla_w_uk, mla_w_uv,
              mlstm_b_i, mlstm_b_f, out_norm_ssm, out_norm_mla, out_norm_mlstm, w_out, ffn_w_gate,
              ffn_w_up, ffn_w_down):
    f32 = jnp.float32
    B_, S, _ = x_prompt.shape
    DB, T, _ = x_sample.shape
    past = page_table.shape[1] * cache_ckv.shape[2]
    pos_p = jnp.arange(S)
    pos_s = past + jnp.arange(T)
    ssm0_p = (jnp.zeros((B_, SSM_GROUPS, SSM_STATE), f32), jnp.zeros((B_, SSM_GROUPS, SSM_STATE), f32))
    mlstm0_p = (jnp.zeros((B_, MLSTM_HEADS, MLSTM_DH, MLSTM_DH), f32),
                jnp.zeros((B_, MLSTM_HEADS, MLSTM_DH), f32), jnp.zeros((B_, MLSTM_HEADS), f32))
    yp, ys = x_prompt, x_sample
    st_p, st_s = [], []
    for l in range(DEPTH):
        p = dict(norm_mix_pre=norm_mix_pre[l], norm_mix_post=norm_mix_post[l], norm_ffn_pre=norm_ffn_pre[l],
                 norm_ffn_post=norm_ffn_post[l], w_in=w_in[l], ssm_a_re=ssm_a_re[l], ssm_a_im=ssm_a_im[l],
                 ssm_b_re=ssm_b_re[l], ssm_b_im=ssm_b_im[l], ssm_c_re=ssm_c_re[l], ssm_c_im=ssm_c_im[l],
                 ssm_d=ssm_d[l], ssm_log_dt=ssm_log_dt[l], ssm_w_glu=ssm_w_glu[l], ssm_b_glu=ssm_b_glu[l],
                 mla_q_norm=mla_q_norm[l], mla_w_uq=mla_w_uq[l], mla_kv_norm=mla_kv_norm[l],
                 mla_w_uk=mla_w_uk[l], mla_w_uv=mla_w_uv[l], mlstm_b_i=mlstm_b_i[l], mlstm_b_f=mlstm_b_f[l],
                 out_norm_ssm=out_norm_ssm[l], out_norm_mla=out_norm_mla[l], out_norm_mlstm=out_norm_mlstm[l],
                 w_out=w_out[l], ffn_w_gate=ffn_w_gate[l], ffn_w_up=ffn_w_up[l], ffn_w_down=ffn_w_down[l])
        ckv_past = cache_ckv[l, page_table].reshape(DB, past, MLA_KV_RANK).astype(f32)
        kr_past = cache_krope[l, page_table].reshape(DB, past, MLA_ROPE).astype(f32)
        yp, sp = decoder_layer(yp, pos_p, ssm0_p, mlstm0_p, None, p)
        ys, ss = decoder_layer(ys, pos_s, (state_ssm_re[l], state_ssm_im[l]),
                               (state_mlstm_C[l], state_mlstm_n[l], state_mlstm_m[l]), (ckv_past, kr_past), p)
        st_p.append(sp)
        st_s.append(ss)
    stk = lambda lst, i: jnp.stack([s[i] for s in lst], axis=0)
    return (yp, ys,
            stk(st_p, 0), stk(st_p, 1), stk(st_s, 0), stk(st_s, 1),
            stk(st_p, 2), stk(st_p, 3), stk(st_s, 2), stk(st_s, 3),
            stk(st_p, 4), stk(st_p, 5), stk(st_p, 6),
            stk(st_s, 4), stk(st_s, 5), stk(st_s, 6))
```

```python
import functools
import math

import numpy as np
import jax
import jax.numpy as jnp
from jax import lax
from jax.experimental import pallas as pl
from jax.experimental.pallas import tpu as pltpu

F32 = jnp.float32
BF16 = jnp.bfloat16

RMS_EPS = 1e-6
ROPE_THETA = 10000.0
MLA_NOPE = 64
MLA_ROPE = 32
HEAD_PAD = 128
ROT = MLA_ROPE // 2

TM_PROJ = 512
TM_FFN = 512
T_SSM = 512
T_ATT = 256
T_MLSTM = 256
TOK_MLSTM_DEC = 16
PAGES_PER_STEP = 16
VMEM_LIMIT = 56 * 1024 * 1024

NT_DIMS = (((1,), (1,)), ((), ()))


def _dot(a, b):
    return jnp.dot(a, b, preferred_element_type=F32)


def _dot_nt(a, b):
    return lax.dot_general(a, b, NT_DIMS, preferred_element_type=F32)


def _rms(x, g):
    return x * lax.rsqrt(jnp.mean(x * x, axis=-1, keepdims=True) + RMS_EPS) * g


def _log_sigmoid(x):
    return jnp.minimum(x, 0.0) - jnp.log1p(jnp.exp(-jnp.abs(x)))


def _rope_lanes(c, cos_t, sin_lo, sin_hi):
    return c * cos_t + pltpu.roll(c, HEAD_PAD - ROT, 1) * sin_lo + pltpu.roll(c, ROT, 1) * sin_hi


def _const_spec(shape):
    nd = len(shape)
    return pl.BlockSpec(shape, lambda *_: (0,) * nd, pipeline_mode=pl.Buffered(1))


def _params(sem):
    return pltpu.CompilerParams(dimension_semantics=sem, vmem_limit_bytes=VMEM_LIMIT)


Z_U, Z_CQ, Z_CKV, Z_MQ, Z_MK, Z_MV, Z_MO, Z_TAIL, Z_END = 0, 256, 640, 896, 1408, 1920, 2432, 2688, 2816
TAIL_IG = MLA_ROPE
TAIL_LF = MLA_ROPE + 4


def _inproj_kernel(x_ref, gpre_ref, wz_ref, wgt_ref, wkt_ref, bcol_ref, brow_ref,
                   tcos_ref, tslo_ref, tshi_ref, qcos_ref, qslo_ref, qshi_ref,
                   qn_ref, wuq_ref, kvn_ref, wkt_top_ref, wkt_bot_ref, wv_ref, vone_ref, vonem_ref,
                   u_ref, qpad_ref, ckv_ref, kr_ref, kpad_ref, vpad_ref,
                   mq_ref, mk_ref, mv_ref, mo_ref, gcol_ref, grow_ref, kT_ref, *, n_heads, scale):
    hb = _rms(x_ref[...], gpre_ref[...]).astype(BF16)
    z = _dot(hb, wz_ref[...])

    u_ref[...] = z[:, Z_U:Z_CQ]
    mq_ref[...] = z[:, Z_MQ:Z_MK].astype(BF16)
    mk_ref[...] = (z[:, Z_MK:Z_MV] * 0.125).astype(BF16)
    mv_ref[...] = (z[:, Z_MV:Z_MO] + vonem_ref[...]).astype(BF16)
    mo_ref[...] = z[:, Z_MO:Z_TAIL]

    tail = z[:, Z_TAIL:Z_END]
    rot = _rope_lanes(tail, tcos_ref[...], tslo_ref[...], tshi_ref[...])
    kr_ref[...] = rot[:, :MLA_ROPE]

    gt = tail + bcol_ref[...]
    lane = lax.broadcasted_iota(jnp.int32, gt.shape, 1)
    gcol_ref[...] = jnp.where((lane >= TAIL_LF) & (lane < TAIL_LF + 4), _log_sigmoid(gt), gt)

    gr = _dot_nt(wgt_ref[...], hb) + brow_ref[...]
    row = lax.broadcasted_iota(jnp.int32, gr.shape, 0)
    grow_ref[...] = jnp.where(row >= 4, _log_sigmoid(gr), gr)

    kT_ref[...] = (_dot_nt(wkt_ref[...], hb) * 0.125).astype(BF16)

    cqn = _rms(z[:, Z_CQ:Z_CKV], qn_ref[...]).astype(BF16)
    qa = _dot(cqn, wuq_ref[...]) * scale
    qcos, qslo, qshi = qcos_ref[...], qslo_ref[...], qshi_ref[...]
    for h in range(n_heads):
        c = qa[:, h * HEAD_PAD:(h + 1) * HEAD_PAD]
        qpad_ref[:, h * HEAD_PAD:(h + 1) * HEAD_PAD] = _rope_lanes(c, qcos, qslo, qshi).astype(BF16)

    ckvn = _rms(z[:, Z_CKV:Z_MQ], kvn_ref[...])
    ckv_ref[...] = ckvn
    cb = ckvn.astype(BF16)
    kpad_ref[...] = (_dot(cb, wkt_top_ref[...]) + _dot(rot.astype(BF16), wkt_bot_ref[...])).astype(BF16)
    vpad_ref[...] = (_dot(cb, wv_ref[...]) + vone_ref[...]).astype(BF16)


def _inproj(x, wl, tables, *, tm, tiles_per_seq):
    n, d = x.shape
    nh = wl['n_heads']
    hp = nh * HEAD_PAD
    grid = (n // tm,)
    row = lambda w: pl.BlockSpec((tm, w), lambda i: (i, 0))
    tab = pl.BlockSpec((tm, HEAD_PAD), lambda i: (i % tiles_per_seq, 0))
    weights = [wl['g_pre'], wl['wz'], wl['wgt'], wl['wkt'], wl['bcol'], wl['brow']]
    mla_w = [wl['qn'], wl['wuq'], wl['kvn'], wl['wk_top'], wl['wk_bot'], wl['wv'], wl['vone'], wl['vonem']]
    in_specs = ([row(d)] + [_const_spec(w.shape) for w in weights] + [tab] * 6
                + [_const_spec(w.shape) for w in mla_w])
    out_shape = [
        jax.ShapeDtypeStruct((n, 256), F32),
        jax.ShapeDtypeStruct((n, hp), BF16),
        jax.ShapeDtypeStruct((n, 256), F32),
        jax.ShapeDtypeStruct((n, MLA_ROPE), F32),
        jax.ShapeDtypeStruct((n, hp), BF16),
        jax.ShapeDtypeStruct((n, hp), BF16),
        jax.ShapeDtypeStruct((n, 512), BF16),
        jax.ShapeDtypeStruct((n, 512), BF16),
        jax.ShapeDtypeStruct((n, 512), BF16),
        jax.ShapeDtypeStruct((n, 256), F32),
        jax.ShapeDtypeStruct((n, HEAD_PAD), F32),
        jax.ShapeDtypeStruct((8, n), F32),
        jax.ShapeDtypeStruct((256, n), BF16),
    ]
    out_specs = [row(256), row(hp), row(256), row(MLA_ROPE), row(hp), row(hp), row(512), row(512), row(512),
                 row(256), row(HEAD_PAD), pl.BlockSpec((8, tm), lambda i: (0, i)),
                 pl.BlockSpec((256, tm), lambda i: (0, i))]
    kern = functools.partial(_inproj_kernel, n_heads=nh, scale=(MLA_NOPE + MLA_ROPE) ** -0.5)
    return pl.pallas_call(kern, out_shape=out_shape, grid=grid, in_specs=in_specs, out_specs=out_specs,
                          compiler_params=_params(("parallel",)), name="inproj")(
        x, *weights, *tables, *mla_w)


def _ssm_discretise(are, aim, ldt):
    dt = jnp.exp(ldt)
    mag = jnp.exp(are * dt)
    lbr = mag * jnp.cos(aim * dt)
    lbi = mag * jnp.sin(aim * dt)
    inv = 1.0 / (are * are + aim * aim)
    fr = ((lbr - 1.0) * are + lbi * aim) * inv
    fi = (lbi * are - (lbr - 1.0) * aim) * inv
    return lbr, lbi, fr, fi


def _ssm_input_matrix(fr, fi, bre, bim):
    return jnp.concatenate([fr * bre - fi * bim, fr * bim + fi * bre], axis=1).astype(BF16)


def _ssm_tail(xcat, u, ccat_ref, d_ref, wglu_ref, bglu_ref, gout_ref):
    y = _dot(xcat.astype(BF16), ccat_ref[...]) + d_ref[...] * u
    y = jax.nn.gelu(y)
    y = y * jax.nn.sigmoid(_dot(y.astype(BF16), wglu_ref[...]) + bglu_ref[...])
    return _rms(y, gout_ref[...])


def _cmul(ar, ai, br, bi):
    return ar * br - ai * bi, ar * bi + ai * br


def _ssm_prompt_kernel(u_ref, are_ref, aim_ref, ldt_ref, bre_ref, bim_ref, ccat_ref, d_ref, wglu_ref, bglu_ref,
                       gout_ref, y_ref, st_ref, x_scr, car_scr, *, tc, ns):
    c = pl.program_id(1)
    lbr, lbi, fr, fi = _ssm_discretise(are_ref[...], aim_ref[...], ldt_ref[...])
    u = u_ref[...]
    x_scr[...] = _dot(u.astype(BF16), _ssm_input_matrix(fr, fi, bre_ref[...], bim_ref[...]))

    @pl.when(c == 0)
    def _():
        car_scr[...] = jnp.zeros_like(car_scr)

    p = [(lbr, lbi)]
    for _ in range(7):
        p.append(_cmul(p[-1][0], p[-1][1], lbr, lbi))
    rowi = lax.broadcasted_iota(jnp.int32, (8, ns), 0)
    pwr = jnp.zeros((8, ns), F32)
    pwi = jnp.zeros((8, ns), F32)
    for j in range(8):
        pwr = jnp.where(rowi == j, p[j][0], pwr)
        pwi = jnp.where(rowi == j, p[j][1], pwi)
    coefs = []
    for dd in (1, 2, 4):
        coefs.append((dd, jnp.where(rowi >= dd, p[dd - 1][0], 0.0), jnp.where(rowi >= dd, p[dd - 1][1], 0.0)))

    def body(g, carry):
        hr, hi = carry
        r0 = pl.multiple_of(g * 8, 8)
        xr = x_scr[pl.ds(r0, 8), 0:ns]
        xi = x_scr[pl.ds(r0, 8), ns:2 * ns]
        for dd, ar, ai in coefs:
            sr = pltpu.roll(xr, dd, 0)
            si = pltpu.roll(xi, dd, 0)
            xr, xi = xr + ar * sr - ai * si, xi + ar * si + ai * sr
        xr = xr + pwr * hr - pwi * hi
        xi = xi + pwr * hi + pwi * hr
        x_scr[pl.ds(r0, 8), 0:ns] = xr
        x_scr[pl.ds(r0, 8), ns:2 * ns] = xi
        return xr[7:8, :], xi[7:8, :]

    hr, hi = lax.fori_loop(0, tc // 8, body, (car_scr[0:1, 0:ns], car_scr[0:1, ns:2 * ns]))
    car_scr[0:1, 0:ns] = hr
    car_scr[0:1, ns:2 * ns] = hi
    st_ref[0:1, 0:ns] = hr
    st_ref[0:1, ns:2 * ns] = hi
    y_ref[...] = _ssm_tail(x_scr[...], u, ccat_ref, d_ref, wglu_ref, bglu_ref, gout_ref)


def _ssm_prompt(u, wl, *, batch, seq, tc):
    n, w = u.shape
    ns = wl['ssm_are'].shape[1]
    nt = seq // tc
    weights = [wl['ssm_are'], wl['ssm_aim'], wl['ssm_ldt'], wl['ssm_bre'], wl['ssm_bim'], wl['ssm_ccat'],
               wl['ssm_d'], wl['ssm_wglu'], wl['ssm_bglu'], wl['g_ssm']]
    kern = functools.partial(_ssm_prompt_kernel, tc=tc, ns=ns)
    y, st = pl.pallas_call(
        kern,
        out_shape=[jax.ShapeDtypeStruct((n, w), F32), jax.ShapeDtypeStruct((batch, 1, 2 * ns), F32)],
        grid=(batch, nt),
        in_specs=[pl.BlockSpec((tc, w), lambda b, c: (b * nt + c, 0))] + [_const_spec(a.shape) for a in weights],
        out_specs=[pl.BlockSpec((tc, w), lambda b, c: (b * nt + c, 0)),
                   pl.BlockSpec((None, 1, 2 * ns), lambda b, c: (b, 0, 0))],
        scratch_shapes=[pltpu.VMEM((tc, 2 * ns), F32), pltpu.VMEM((8, 2 * ns), F32)],
        compiler_params=_params(("parallel", "arbitrary")), name="ssm_prompt")(u, *weights)
    return y, st[:, 0, :ns], st[:, 0, ns:]


def _ssm_decode_kernel(u_ref, h0r_ref, h0i_ref, are_ref, aim_ref, ldt_ref, bre_ref, bim_ref, ccat_ref, d_ref,
                       wglu_ref, bglu_ref, gout_ref, y_ref, hr_ref, hi_ref, *, ns):
    lbr, lbi, fr, fi = _ssm_discretise(are_ref[...], aim_ref[...], ldt_ref[...])
    u = u_ref[...]
    bu = _dot(u.astype(BF16), _ssm_input_matrix(fr, fi, bre_ref[...], bim_ref[...]))
    h0r, h0i = h0r_ref[...], h0i_ref[...]
    hr = bu[:, 0:ns] + lbr * h0r - lbi * h0i
    hi = bu[:, ns:2 * ns] + lbr * h0i + lbi * h0r
    hr_ref[...] = hr
    hi_ref[...] = hi
    y_ref[...] = _ssm_tail(jnp.concatenate([hr, hi], axis=1), u, ccat_ref, d_ref, wglu_ref, bglu_ref, gout_ref)


def _ssm_decode(u, h0r, h0i, wl):
    n, w = u.shape
    ns = wl['ssm_are'].shape[1]
    weights = [wl['ssm_are'], wl['ssm_aim'], wl['ssm_ldt'], wl['ssm_bre'], wl['ssm_bim'], wl['ssm_ccat'],
               wl['ssm_d'], wl['ssm_wglu'], wl['ssm_bglu'], wl['g_ssm']]
    kern = functools.partial(_ssm_decode_kernel, ns=ns)
    return pl.pallas_call(
        kern,
        out_shape=[jax.ShapeDtypeStruct((n, w), F32), jax.ShapeDtypeStruct((n, ns), F32),
                   jax.ShapeDtypeStruct((n, ns), F32)],
        compiler_params=pltpu.CompilerParams(vmem_limit_bytes=VMEM_LIMIT), name="ssm_decode")(u, h0r, h0i, *weights)


def _attn_kernel(q_ref, k_ref, v_ref, o_ref, m_scr, acc_scr, *, n_heads, tq, dv):
    qi = pl.program_id(1)
    rowi = lax.broadcasted_iota(jnp.int32, (tq, tq), 0)
    coli = lax.broadcasted_iota(jnp.int32, (tq, tq), 1)
    causal = coli <= rowi
    d0 = pl.multiple_of(qi * tq, tq)
    outs = []
    for h in range(n_heads):
        hs = slice(h * HEAD_PAD, (h + 1) * HEAD_PAD)
        qh = q_ref[:, hs]
        s = jnp.where(causal, _dot_nt(qh, k_ref[pl.ds(d0, tq), hs]), -jnp.inf)
        m = jnp.max(s, axis=1, keepdims=True)
        m_scr[...] = m
        acc_scr[...] = _dot(jnp.exp(s - m).astype(BF16), v_ref[pl.ds(d0, tq), hs])

        def body(j, _):
            k0 = pl.multiple_of(j * tq, tq)
            sj = _dot_nt(qh, k_ref[pl.ds(k0, tq), hs])
            m_old = m_scr[...]
            m_new = jnp.maximum(m_old, jnp.max(sj, axis=1, keepdims=True))
            pj = jnp.exp(sj - m_new).astype(BF16)
            acc_scr[...] = jnp.exp(m_old - m_new) * acc_scr[...] + _dot(pj, v_ref[pl.ds(k0, tq), hs])
            m_scr[...] = m_new
            return 0

        lax.fori_loop(0, qi, body, 0)
        acc = acc_scr[...]
        outs.append(acc[:, :dv] / acc[:, dv:dv + 1])
    o_ref[...] = jnp.concatenate(outs, axis=1)


def _attn_prompt(q, k, v, *, batch, seq, n_heads, tq, dv):
    n, hp = q.shape
    nq = seq // tq
    kern = functools.partial(_attn_kernel, n_heads=n_heads, tq=tq, dv=dv)
    return pl.pallas_call(
        kern,
        out_shape=jax.ShapeDtypeStruct((n, n_heads * dv), F32),
        grid=(batch, nq),
        in_specs=[pl.BlockSpec((tq, hp), lambda b, i: (b * nq + i, 0)),
                  pl.BlockSpec((seq, hp), lambda b, i: (b, 0)),
                  pl.BlockSpec((seq, hp), lambda b, i: (b, 0))],
        out_specs=pl.BlockSpec((tq, n_heads * dv), lambda b, i: (b * nq + i, 0)),
        scratch_shapes=[pltpu.VMEM((tq, 1), F32), pltpu.VMEM((tq, HEAD_PAD), F32)],
        compiler_params=_params(("parallel", "arbitrary")), name="attn_prompt")(q, k, v)


def _qlat_kernel(q_ref, wukt_ref, esel_ref, qlat_ref, qrope_ref, *, n_heads):
    for h in range(n_heads):
        qh = q_ref[:, h * HEAD_PAD:(h + 1) * HEAD_PAD]
        qlat_ref[h] = _dot(qh[:, :MLA_NOPE], wukt_ref[h]).astype(BF16)
        qrope_ref[h] = _dot(qh, esel_ref[...]).astype(BF16)


def _qlat(qpad, wl):
    n, _ = qpad.shape
    nh = wl['n_heads']
    r = wl['wukt'].shape[2]
    kern = functools.partial(_qlat_kernel, n_heads=nh)
    return pl.pallas_call(
        kern,
        out_shape=[jax.ShapeDtypeStruct((nh, n, r), BF16), jax.ShapeDtypeStruct((nh, n, HEAD_PAD), BF16)],
        compiler_params=pltpu.CompilerParams(vmem_limit_bytes=VMEM_LIMIT), name="qlat")(
        qpad, wl['wukt'], wl['esel'])


def _dattn_kernel(pt_ref, qlat_ref, qrope_ref, cnew_ref, knew_ref, *refs, pps):
    ckv_refs = refs[:pps]
    kr_refs = refs[pps:2 * pps]
    o_ref, m_scr, l_scr, acc_scr = refs[2 * pps:]
    c = pl.program_id(1)
    ql = qlat_ref[...]
    qr = qrope_ref[...][:, :MLA_ROPE]

    @pl.when(c == 0)
    def _():
        cn = cnew_ref[...].astype(BF16).astype(F32)
        kn = knew_ref[...].astype(BF16).astype(F32)
        s_self = (jnp.sum(ql.astype(F32) * cn, axis=1, keepdims=True)
                  + jnp.sum(qr.astype(F32) * kn, axis=1, keepdims=True))
        m_scr[...] = s_self
        l_scr[...] = jnp.ones_like(l_scr)
        acc_scr[...] = jnp.broadcast_to(cn, acc_scr.shape)

    pages = [r[...].astype(BF16) for r in ckv_refs]
    s = jnp.concatenate(
        [_dot_nt(ql, pages[j]) + _dot_nt(qr, kr_refs[j][...].astype(BF16)) for j in range(pps)], axis=1)
    m_old = m_scr[...]
    m_new = jnp.maximum(m_old, jnp.max(s, axis=1, keepdims=True))
    alpha = jnp.exp(m_old - m_new)
    p = jnp.exp(s - m_new)
    l_scr[...] = alpha * l_scr[...] + jnp.sum(p, axis=1, keepdims=True)
    pb = p.astype(BF16)
    page = pages[0].shape[0]
    pv = _dot(pb[:, 0:page], pages[0])
    for j in range(1, pps):
        pv = pv + _dot(pb[:, j * page:(j + 1) * page], pages[j])
    acc_scr[...] = alpha * acc_scr[...] + pv
    m_scr[...] = m_new

    @pl.when(c == pl.num_programs(1) - 1)
    def _():
        o_ref[...] = acc_scr[...] / l_scr[...]


def _dattn(qlat, qrope, cnew, knew, cache_ckv, cache_krope, pt_flat, *, layer, n_pages, pps):
    nb, nh, r = qlat.shape
    page = cache_ckv.shape[2]
    nc = n_pages // pps

    def page_spec(width, j):
        return pl.BlockSpec((None, None, page, width),
                            lambda s, c, pt: (layer, pt[s * n_pages + c * pps + j], 0, 0))

    per_seq = lambda shape: pl.BlockSpec((None,) + shape, lambda s, c, pt: (s, 0, 0))
    in_specs = ([per_seq((nh, r)), per_seq((nh, HEAD_PAD)), per_seq((1, r)), per_seq((1, MLA_ROPE))]
                + [page_spec(r, j) for j in range(pps)] + [page_spec(MLA_ROPE, j) for j in range(pps)])
    grid_spec = pltpu.PrefetchScalarGridSpec(
        num_scalar_prefetch=1, grid=(nb, nc), in_specs=in_specs,
        out_specs=per_seq((nh, r)),
        scratch_shapes=[pltpu.VMEM((nh, 1), F32), pltpu.VMEM((nh, 1), F32), pltpu.VMEM((nh, r), F32)])
    kern = functools.partial(_dattn_kernel, pps=pps)
    return pl.pallas_call(
        kern, out_shape=jax.ShapeDtypeStruct((nb, nh, r), F32), grid_spec=grid_spec,
        compiler_params=_params(("parallel", "arbitrary")), name="attn_decode")(
        pt_flat, qlat, qrope, cnew, knew, *([cache_ckv] * pps), *([cache_krope] * pps))


def _uv_kernel(o_ref, wuv_ref, y_ref, *, n_heads):
    y_ref[...] = jnp.concatenate([_dot(o_ref[h].astype(BF16), wuv_ref[h]) for h in range(n_heads)], axis=1)


def _uv(olat_hm, wl):
    nh, n, _ = olat_hm.shape
    dv = wl['wuv_h'].shape[2]
    kern = functools.partial(_uv_kernel, n_heads=nh)
    return pl.pallas_call(kern, out_shape=jax.ShapeDtypeStruct((n, nh * dv), F32),
                          compiler_params=pltpu.CompilerParams(vmem_limit_bytes=VMEM_LIMIT), name="uv")(
        olat_hm, wl['wuv_h'])


def _split3(x):
    hi = x.astype(BF16)
    r1 = x - hi.astype(F32)
    mid = r1.astype(BF16)
    lo = (r1 - mid.astype(F32)).astype(BF16)
    return hi, mid, lo


def _mlstm_prompt_kernel(q_ref, k_ref, kT_ref, v_ref, mo_ref, gcol_ref, grow_ref, gout_ref,
                         y_ref, cn_ref, m_ref, cn_scr, m_scr, *, n_heads, tl, dh):
    c = pl.program_id(1)

    @pl.when(c == 0)
    def _():
        cn_scr[...] = jnp.zeros_like(cn_scr)
        m_scr[...] = jnp.zeros_like(m_scr)

    rowi = lax.broadcasted_iota(jnp.int32, (tl, tl), 0)
    coli = lax.broadcasted_iota(jnp.int32, (tl, tl), 1)
    causal = coli <= rowi
    tril = causal.astype(BF16)
    triu = (rowi <= coli).astype(BF16)
    gcol = gcol_ref[...]
    grow = grow_ref[...]
    bcol = sum(_dot(tril, part) for part in _split3(gcol))
    brow = sum(_dot(part, triu) for part in _split3(grow))
    outs = []
    for h in range(n_heads):
        hs = slice(h * HEAD_PAD, (h + 1) * HEAD_PAD)
        qh, kh, vh = q_ref[:, hs], k_ref[:, hs], v_ref[:, hs]
        bc = bcol[:, TAIL_LF + h:TAIL_LF + h + 1]
        br = brow[4 + h:5 + h, :]
        ir = grow[h:h + 1, :]
        m_prev = m_scr[h:h + 1, 0:1]
        lw = jnp.where(causal, bc - (br - ir), -jnp.inf)
        m_inter = bc + m_prev
        mt = jnp.maximum(m_inter, jnp.max(lw, axis=1, keepdims=True))
        w = jnp.exp(lw - mt)
        g = jnp.exp(m_inter - mt)
        qk = _dot_nt(qh, kh) * w
        cn = cn_scr[h]
        tot = g * _dot(qh[:, :dh], cn.astype(BF16)) + _dot(qk.astype(BF16), vh)
        den = tot[:, dh:dh + 1]
        outs.append(tot[:, :dh] / jnp.maximum(jnp.abs(den), jnp.exp(-mt)))
        w_end = w[tl - 1:tl, :]
        g_end = g[tl - 1:tl, :]
        kw = (kT_ref[h * dh:(h + 1) * dh, :].astype(F32) * w_end).astype(BF16)
        cn_scr[h] = g_end * cn + _dot(kw, vh)
        m_scr[h:h + 1, :] = jnp.broadcast_to(mt[tl - 1:tl, :], (1, m_scr.shape[1]))
    hcat = jnp.concatenate(outs, axis=1)
    y_ref[...] = _rms(hcat * jax.nn.sigmoid(mo_ref[...]), gout_ref[...])
    cn_ref[...] = cn_scr[...]
    m_ref[...] = m_scr[...]


def _mlstm_prompt(mq, mk, kT, mv, mo, gcol, grow, wl, *, batch, seq, tl):
    n = mq.shape[0]
    nh = wl['m_heads']
    dh = mo.shape[1] // nh
    nt = seq // tl
    tok = lambda w: pl.BlockSpec((tl, w), lambda b, c: (b * nt + c, 0))
    kern = functools.partial(_mlstm_prompt_kernel, n_heads=nh, tl=tl, dh=dh)
    y, cn, m = pl.pallas_call(
        kern,
        out_shape=[jax.ShapeDtypeStruct((n, nh * dh), F32),
                   jax.ShapeDtypeStruct((batch, nh, dh, HEAD_PAD), F32),
                   jax.ShapeDtypeStruct((batch, 8, HEAD_PAD), F32)],
        grid=(batch, nt),
        in_specs=[tok(nh * HEAD_PAD), tok(nh * HEAD_PAD),
                  pl.BlockSpec((nh * dh, tl), lambda b, c: (0, b * nt + c)),
                  tok(nh * HEAD_PAD), tok(nh * dh), tok(HEAD_PAD),
                  pl.BlockSpec((8, tl), lambda b, c: (0, b * nt + c)),
                  _const_spec(wl['g_mlstm'].shape)],
        out_specs=[tok(nh * dh),
                   pl.BlockSpec((None, nh, dh, HEAD_PAD), lambda b, c: (b, 0, 0, 0)),
                   pl.BlockSpec((None, 8, HEAD_PAD), lambda b, c: (b, 0, 0))],
        scratch_shapes=[pltpu.VMEM((nh, dh, HEAD_PAD), F32), pltpu.VMEM((8, HEAD_PAD), F32)],
        compiler_params=_params(("parallel", "arbitrary")), name="mlstm_prompt")(
        mq, mk, kT, mv, mo, gcol, grow, wl['g_mlstm'])
    return y, cn[..., :dh], cn[..., dh], m[:, :nh, 0]


def _mlstm_decode_kernel(q_ref, k_ref, v_ref, mo_ref, gcol_ref, c0_ref, n0_ref, m0_ref, gout_ref,
                         y_ref, c_ref, n_ref, m_ref, q_scr, k_scr, v_scr, h_scr, *, n_heads, tb, dh):
    q_scr[...] = q_ref[...].astype(F32)
    k_scr[...] = k_ref[...].astype(F32)
    v_scr[...] = v_ref[...].astype(F32)
    eye = (lax.broadcasted_iota(jnp.int32, (dh, dh), 0) == lax.broadcasted_iota(jnp.int32, (dh, dh), 1)).astype(BF16)
    lane = lax.broadcasted_iota(jnp.int32, (1, HEAD_PAD), 1)

    def body(i, _):
        gates = gcol_ref[pl.ds(i, 1), :]
        m0row = m0_ref[pl.ds(i, 1), :]
        qrow = q_scr[pl.ds(i, 1), :]
        krow = k_scr[pl.ds(i, 1), :]
        vrow = v_scr[pl.ds(i, 1), :]
        mrow = jnp.zeros((1, HEAD_PAD), F32)
        hs = []
        for h in range(n_heads):
            qh = qrow[:, h * HEAD_PAD:h * HEAD_PAD + dh]
            kh = krow[:, h * HEAD_PAD:h * HEAD_PAD + dh]
            vh = vrow[:, h * HEAD_PAD:h * HEAD_PAD + dh]
            ig = gates[:, TAIL_IG + h:TAIL_IG + h + 1]
            lf = gates[:, TAIL_LF + h:TAIL_LF + h + 1]
            m_inter = lf + m0row[:, h:h + 1]
            mt = jnp.maximum(m_inter, ig)
            w = jnp.exp(ig - mt)
            g = jnp.exp(m_inter - mt)
            qk = jnp.sum(qh * kh, axis=1, keepdims=True) * w
            c0 = c0_ref[i, h]
            n0 = n0_ref[i, pl.ds(h, 1), :]
            qc = _dot(jnp.broadcast_to(qh, (8, dh)).astype(BF16), c0.astype(BF16))[0:1, :]
            qn = jnp.sum(qh * n0.astype(BF16).astype(F32), axis=1, keepdims=True)
            num = g * qc + qk * vh
            den = g * qn + qk
            hs.append(num / jnp.maximum(jnp.abs(den), jnp.exp(-mt)))
            kcol = _dot_nt(eye, jnp.broadcast_to(kh, (8, dh)).astype(BF16))[:, 0:1]
            c_ref[i, h] = g * c0 + (w * kcol) * vh
            n_ref[i, pl.ds(h, 1), :] = g * n0 + w * kh
            mrow = jnp.where(lane == h, mt, mrow)
        h_scr[pl.ds(i, 1), :] = jnp.concatenate(hs, axis=1)
        m_ref[pl.ds(i, 1), :] = mrow
        return 0

    lax.fori_loop(0, tb, body, 0)
    y_ref[...] = _rms(h_scr[...] * jax.nn.sigmoid(mo_ref[...]), gout_ref[...])


def _mlstm_decode(mq, mk, mv, mo, gcol, c0, n0, m0, wl, *, tb):
    n = mq.shape[0]
    nh = wl['m_heads']
    dh = mo.shape[1] // nh
    m0p = jnp.pad(m0, ((0, 0), (0, HEAD_PAD - nh)))
    tok = lambda w: pl.BlockSpec((tb, w), lambda i: (i, 0))
    kern = functools.partial(_mlstm_decode_kernel, n_heads=nh, tb=tb, dh=dh)
    y, c, nn, m = pl.pallas_call(
        kern,
        out_shape=[jax.ShapeDtypeStruct((n, nh * dh), F32), jax.ShapeDtypeStruct(c0.shape, F32),
                   jax.ShapeDtypeStruct(n0.shape, F32), jax.ShapeDtypeStruct((n, HEAD_PAD), F32)],
        grid=(n // tb,),
        in_specs=[tok(nh * HEAD_PAD), tok(nh * HEAD_PAD), tok(nh * HEAD_PAD), tok(nh * dh), tok(HEAD_PAD),
                  pl.BlockSpec((tb, nh, dh, dh), lambda i: (i, 0, 0, 0)),
                  pl.BlockSpec((tb, nh, dh), lambda i: (i, 0, 0)), tok(HEAD_PAD),
                  _const_spec(wl['g_mlstm'].shape)],
        out_specs=[tok(nh * dh), pl.BlockSpec((tb, nh, dh, dh), lambda i: (i, 0, 0, 0)),
                   pl.BlockSpec((tb, nh, dh), lambda i: (i, 0, 0)), tok(HEAD_PAD)],
        scratch_shapes=[pltpu.VMEM((tb, nh * HEAD_PAD), F32)] * 3 + [pltpu.VMEM((tb, nh * dh), F32)],
        compiler_params=_params(("parallel",)), name="mlstm_decode")(
        mq, mk, mv, mo, gcol, c0, n0, m0p, wl['g_mlstm'])
    return y, c, nn, m[:, :nh]


def _outffn_kernel(x_ref, ya_ref, yb_ref, yc_ref, gmla_ref, wout_ref, gpost_ref, gfpre_ref, wg_ref, wu_ref, wd_ref,
                   gfpost_ref, o_ref, *, ff_chunk):
    yb = _rms(yb_ref[...], gmla_ref[...])
    y = jnp.concatenate([ya_ref[...], yb, yc_ref[...]], axis=1).astype(BF16)
    x1 = x_ref[...] + _rms(_dot(y, wout_ref[...]), gpost_ref[...])
    h2 = _rms(x1, gfpre_ref[...]).astype(BF16)
    dff = wg_ref.shape[1]
    f = None
    for c0 in range(0, dff, ff_chunk):
        a = jax.nn.silu(_dot(h2, wg_ref[:, c0:c0 + ff_chunk])) * _dot(h2, wu_ref[:, c0:c0 + ff_chunk])
        part = _dot(a.astype(BF16), wd_ref[c0:c0 + ff_chunk, :])
        f = part if f is None else f + part
    o_ref[...] = x1 + _rms(f, gfpost_ref[...])


def _outffn(x, ya, yb, yc, wl, *, tm):
    n, d = x.shape
    weights = [wl['g_mla'], wl['wout'], wl['g_post'], wl['g_fpre'], wl['wg'], wl['wu'], wl['wd'], wl['g_fpost']]
    row = lambda w: pl.BlockSpec((tm, w), lambda i: (i, 0))
    dff = wl['wg'].shape[1]
    kern = functools.partial(_outffn_kernel, ff_chunk=dff // 2)
    return pl.pallas_call(
        kern, out_shape=jax.ShapeDtypeStruct((n, d), F32), grid=(n // tm,),
        in_specs=[row(d), row(ya.shape[1]), row(yb.shape[1]), row(yc.shape[1])]
        + [_const_spec(w.shape) for w in weights],
        out_specs=row(d), compiler_params=_params(("parallel",)), name="outffn")(x, ya, yb, yc, *weights)


def _rope_tables(positions):
    pos = np.asarray(positions, dtype=np.float64)[:, None]
    inv = ROPE_THETA ** (-np.arange(ROT, dtype=np.float64) / ROT)
    cos, sin = np.cos(pos * inv), np.sin(pos * inv)
    n = pos.shape[0]

    def lanes(off, nope_one):
        c = np.zeros((n, HEAD_PAD))
        lo = np.zeros((n, HEAD_PAD))
        hi = np.zeros((n, HEAD_PAD))
        if nope_one:
            c[:, :off] = 1.0
        c[:, off:off + ROT] = cos
        c[:, off + ROT:off + 2 * ROT] = cos
        lo[:, off:off + ROT] = -sin
        hi[:, off + ROT:off + 2 * ROT] = sin
        return [jnp.asarray(a, dtype=F32) for a in (c, lo, hi)]

    return lanes(0, False) + lanes(MLA_NOPE, True)


def _pad_heads(w, n_heads):
    rows = w.shape[0]
    w3 = w.reshape(rows, n_heads, -1)
    return jnp.pad(w3, ((0, 0), (0, 0), (0, HEAD_PAD - w3.shape[2]))).reshape(rows, n_heads * HEAD_PAD)


def _prep_layer(l, p):
    f = lambda name: p[name][l]
    row = lambda a: a.reshape(1, -1).astype(F32)
    w_in = f('w_in')
    d = w_in.shape[0]
    nh = p['mla_w_uk'].shape[2]
    mh = p['mlstm_b_i'].shape[1]
    o = 0
    cols = {}
    for name, width in (('u', 256), ('cq', 384), ('ckv', 256), ('kr', MLA_ROPE), ('mq', 256), ('mk', 256),
                        ('mv', 256), ('mo', 256), ('mi', mh), ('mf', mh)):
        cols[name] = w_in[:, o:o + width]
        o += width
    tail = jnp.concatenate([cols['kr'], cols['mi'], cols['mf'],
                            jnp.zeros((d, HEAD_PAD - MLA_ROPE - 2 * mh), F32)], axis=1)
    wz = jnp.concatenate([cols['u'], cols['cq'], cols['ckv'], _pad_heads(cols['mq'], mh), _pad_heads(cols['mk'], mh),
                          _pad_heads(cols['mv'], mh), cols['mo'], tail], axis=1).astype(BF16)
    bcol = jnp.zeros((1, HEAD_PAD), F32).at[0, TAIL_IG:TAIL_IG + mh].set(f('mlstm_b_i'))
    bcol = bcol.at[0, TAIL_LF:TAIL_LF + mh].set(f('mlstm_b_f'))
    brow = jnp.concatenate([f('mlstm_b_i'), f('mlstm_b_f')]).reshape(2 * mh, 1)
    one_lane = lambda n: jnp.zeros((n, HEAD_PAD), F32).at[:, MLA_NOPE].set(1.0).reshape(1, n * HEAD_PAD)

    w_uk = f('mla_w_uk')
    w_uv = f('mla_w_uv')
    r = w_uk.shape[0]
    wk_top = jnp.pad(w_uk, ((0, 0), (0, 0), (0, HEAD_PAD - MLA_NOPE))).reshape(r, nh * HEAD_PAD)
    place = jnp.zeros((HEAD_PAD, nh, HEAD_PAD), F32)
    e = jnp.arange(MLA_ROPE)
    place = place.at[e, :, MLA_NOPE + e].set(1.0)
    esel = jnp.zeros((HEAD_PAD, HEAD_PAD), F32).at[MLA_NOPE + e, e].set(1.0)

    g_ = p['ssm_a_re'].shape[1]
    eye_g = jnp.eye(g_, dtype=F32)
    bre = jnp.einsum('gpc,gh->gchp', f('ssm_b_re'), eye_g).reshape(256, -1)
    bim = jnp.einsum('gpc,gh->gchp', f('ssm_b_im'), eye_g).reshape(256, -1)
    cre = jnp.einsum('gcp,gh->gphc', f('ssm_c_re'), eye_g).reshape(-1, 256)
    cim = jnp.einsum('gcp,gh->gphc', f('ssm_c_im'), eye_g).reshape(-1, 256)
    ns = bre.shape[1]
    return dict(
        n_heads=nh, m_heads=mh,
        g_pre=row(f('norm_mix_pre')), wz=wz,
        wgt=jnp.concatenate([cols['mi'], cols['mf']], axis=1).T.astype(BF16),
        wkt=cols['mk'].T.astype(BF16), bcol=bcol, brow=brow,
        qn=row(f('mla_q_norm')), wuq=_pad_heads(f('mla_w_uq'), nh).astype(BF16), kvn=row(f('mla_kv_norm')),
        wk_top=wk_top.astype(BF16), wk_bot=place.reshape(HEAD_PAD, nh * HEAD_PAD).astype(BF16),
        wv=jnp.pad(w_uv, ((0, 0), (0, 0), (0, HEAD_PAD - w_uv.shape[2]))).reshape(r, nh * HEAD_PAD).astype(BF16),
        vone=one_lane(nh), vonem=one_lane(mh),
        wukt=w_uk.transpose(1, 2, 0).astype(BF16),
        esel=esel.astype(BF16),
        wuv_h=w_uv.transpose(1, 0, 2).astype(BF16),
        ssm_are=row(f('ssm_a_re')), ssm_aim=row(f('ssm_a_im')),
        ssm_ldt=jnp.repeat(f('ssm_log_dt'), ns // g_).reshape(1, ns),
        ssm_bre=bre, ssm_bim=bim, ssm_ccat=jnp.concatenate([cre, -cim], axis=0).astype(BF16),
        ssm_d=row(f('ssm_d')), ssm_wglu=f('ssm_w_glu').astype(BF16), ssm_bglu=row(f('ssm_b_glu')),
        g_ssm=row(f('out_norm_ssm')), g_mla=row(f('out_norm_mla')), g_mlstm=row(f('out_norm_mlstm')),
        wout=f('w_out').astype(BF16), g_post=row(f('norm_mix_post')), g_fpre=row(f('norm_ffn_pre')),
        wg=f('ffn_w_gate').astype(BF16), wu=f('ffn_w_up').astype(BF16), wd=f('ffn_w_down').astype(BF16),
        g_fpost=row(f('norm_ffn_post')))


def kernel(x_prompt, x_sample, cache_ckv, cache_krope, page_table, state_ssm_re, state_ssm_im, state_mlstm_C, state_mlstm_n, state_mlstm_m, norm_mix_pre, norm_mix_post, norm_ffn_pre, norm_ffn_post, w_in, ssm_a_re, ssm_a_im, ssm_b_re, ssm_b_im, ssm_c_re, ssm_c_im, ssm_d, ssm_log_dt, ssm_w_glu, ssm_b_glu, mla_q_norm, mla_w_uq, mla_kv_norm, mla_w_uk, mla_w_uv, mlstm_b_i, mlstm_b_f, out_norm_ssm, out_norm_mla, out_norm_mlstm, w_out, ffn_w_gate, ffn_w_up, ffn_w_down):
    p = dict(norm_mix_pre=norm_mix_pre, norm_mix_post=norm_mix_post, norm_ffn_pre=norm_ffn_pre,
             norm_ffn_post=norm_ffn_post, w_in=w_in, ssm_a_re=ssm_a_re, ssm_a_im=ssm_a_im, ssm_b_re=ssm_b_re,
             ssm_b_im=ssm_b_im, ssm_c_re=ssm_c_re, ssm_c_im=ssm_c_im, ssm_d=ssm_d, ssm_log_dt=ssm_log_dt,
             ssm_w_glu=ssm_w_glu, ssm_b_glu=ssm_b_glu, mla_q_norm=mla_q_norm, mla_w_uq=mla_w_uq,
             mla_kv_norm=mla_kv_norm, mla_w_uk=mla_w_uk, mla_w_uv=mla_w_uv, mlstm_b_i=mlstm_b_i,
             mlstm_b_f=mlstm_b_f, out_norm_ssm=out_norm_ssm, out_norm_mla=out_norm_mla,
             out_norm_mlstm=out_norm_mlstm, w_out=w_out, ffn_w_gate=ffn_w_gate, ffn_w_up=ffn_w_up,
             ffn_w_down=ffn_w_down)
    depth = w_in.shape[0]
    bsz, seq, d = x_prompt.shape
    db, dec_t, _ = x_sample.shape
    assert dec_t == 1, "the sample group advances one token per sequence"
    n_pages, page = page_table.shape[1], cache_ckv.shape[2]
    past = n_pages * page
    g_, ns_g = ssm_a_re.shape[1], ssm_a_re.shape[2]
    ns = g_ * ns_g

    tm_p = min(TM_PROJ, seq)
    tm_f = min(TM_FFN, seq)
    tc = min(T_SSM, seq)
    tq = min(T_ATT, seq)
    tl = min(T_MLSTM, seq)
    tb = min(TOK_MLSTM_DEC, db)
    pps = min(PAGES_PER_STEP, n_pages)
    assert seq % tm_p == 0 and seq % tm_f == 0 and seq % tc == 0 and seq % tq == 0 and seq % tl == 0
    assert db % tb == 0 and n_pages % pps == 0

    tabs_p = _rope_tables(np.arange(seq))
    tabs_s = _rope_tables(np.full((db,), past))
    pt_flat = page_table.reshape(-1).astype(jnp.int32)

    xp = x_prompt.reshape(bsz * seq, d)
    xs = x_sample.reshape(db, d)
    outs_p, outs_s = [], []
    for l in range(depth):
        wl = _prep_layer(l, p)
        nh, mh = wl['n_heads'], wl['m_heads']
        dv = wl['wuv_h'].shape[2]

        (u, qpad, ckvn, kr, kpad, vpad, mq, mk, mv, mo, gcol, grow, kT) = _inproj(
            xp, wl, tabs_p, tm=tm_p, tiles_per_seq=seq // tm_p)
        ya, s_re, s_im = _ssm_prompt(u, wl, batch=bsz, seq=seq, tc=tc)
        yb = _attn_prompt(qpad, kpad, vpad, batch=bsz, seq=seq, n_heads=nh, tq=tq, dv=dv)
        yc, c_p, n_p, m_p = _mlstm_prompt(mq, mk, kT, mv, mo, gcol, grow, wl, batch=bsz, seq=seq, tl=tl)
        xp = _outffn(xp, ya, yb, yc, wl, tm=tm_f)
        outs_p.append((ckvn.reshape(bsz, seq, -1), kr.reshape(bsz, seq, -1),
                       s_re.reshape(bsz, g_, ns_g), s_im.reshape(bsz, g_, ns_g), c_p, n_p, m_p))

        (u, qpad, ckvn, kr, _, _, mq, mk, mv, mo, gcol, _, _) = _inproj(xs, wl, tabs_s, tm=db, tiles_per_seq=1)
        ya, h_re, h_im = _ssm_decode(u, state_ssm_re[l].reshape(db, ns), state_ssm_im[l].reshape(db, ns), wl)
        qlat, qrope = _qlat(qpad, wl)
        olat = _dattn(qlat.transpose(1, 0, 2), qrope.transpose(1, 0, 2), ckvn.reshape(db, 1, -1),
                      kr.reshape(db, 1, -1), cache_ckv, cache_krope, pt_flat, layer=l, n_pages=n_pages, pps=pps)
        yb = _uv(olat.transpose(1, 0, 2), wl)
        yc, c_s, n_s, m_s = _mlstm_decode(mq, mk, mv, mo, gcol, state_mlstm_C[l], state_mlstm_n[l],
                                          state_mlstm_m[l], wl, tb=tb)
        xs = _outffn(xs, ya, yb, yc, wl, tm=db)
        outs_s.append((ckvn.reshape(db, 1, -1), kr.reshape(db, 1, -1),
                       h_re.reshape(db, g_, ns_g), h_im.reshape(db, g_, ns_g), c_s, n_s, m_s))

    stk = lambda lst, i: jnp.stack([s[i] for s in lst], axis=0)
    return (xp.reshape(bsz, seq, d), xs.reshape(db, 1, d),
            stk(outs_p, 0), stk(outs_p, 1), stk(outs_s, 0), stk(outs_s, 1),
            stk(outs_p, 2), stk(outs_p, 3), stk(outs_s, 2), stk(outs_s, 3),
            stk(outs_p, 4), stk(outs_p, 5), stk(outs_p, 6),
            stk(outs_s, 4), stk(outs_s, 5), stk(outs_s, 6))
```

```python
import functools
import math

import numpy as np
import jax
import jax.numpy as jnp
from jax import lax
from jax.experimental import pallas as pl
from jax.experimental.pallas import tpu as pltpu

F32 = jnp.float32
BF16 = jnp.bfloat16

RMS_EPS = 1e-6
ROPE_THETA = 10000.0
MLA_NOPE = 64
MLA_ROPE = 32
HEAD_PAD = 128
ROT = MLA_ROPE // 2

TM_PROJ = 512
TM_FFN = 512
T_SSM = 512
T_ATT = 256
T_MLSTM = 256
TOK_MLSTM_DEC = 16
PAGES_PER_STEP = 32
VMEM_LIMIT = 56 * 1024 * 1024

NT_DIMS = (((1,), (1,)), ((), ()))


def _dot(a, b):
    return jnp.dot(a, b, preferred_element_type=F32)


def _dot_nt(a, b):
    return lax.dot_general(a, b, NT_DIMS, preferred_element_type=F32)


def _rms(x, g):
    return x * lax.rsqrt(jnp.mean(x * x, axis=-1, keepdims=True) + RMS_EPS) * g


def _log_sigmoid(x):
    return jnp.minimum(x, 0.0) - jnp.log1p(jnp.exp(-jnp.abs(x)))


def _rope_lanes(c, cos_t, sin_lo, sin_hi):
    return c * cos_t + pltpu.roll(c, HEAD_PAD - ROT, 1) * sin_lo + pltpu.roll(c, ROT, 1) * sin_hi


def _const_spec(shape):
    nd = len(shape)
    return pl.BlockSpec(shape, lambda *_: (0,) * nd, pipeline_mode=pl.Buffered(1))


def _params(sem):
    return pltpu.CompilerParams(dimension_semantics=sem, vmem_limit_bytes=VMEM_LIMIT)


Z_U, Z_CQ, Z_CKV, Z_MQ, Z_MK, Z_MV, Z_MO, Z_TAIL, Z_END = 0, 256, 640, 896, 1408, 1920, 2432, 2688, 2816
TAIL_IG = MLA_ROPE
TAIL_LF = MLA_ROPE + 4


def _inproj_kernel(x_ref, gpre_ref, wz_ref, wgt_ref, wkt_ref, bcol_ref, brow_ref,
                   tcos_ref, tslo_ref, tshi_ref, qcos_ref, qsin_ref,
                   qn_ref, wuqt_ref, kvn_ref, wkt_top_ref, wkt_bot_ref, wvt_ref, vone_ref, vonem_ref,
                   u_ref, qT_ref, ckv_ref, kr_ref, kpad_ref, vT_ref,
                   mq_ref, mk_ref, mv_ref, mo_ref, gcol_ref, grow_ref, kT_ref, *, n_heads, scale, tk):
    hb = _rms(x_ref[...], gpre_ref[...]).astype(BF16)
    z = _dot(hb, wz_ref[...])

    u_ref[...] = z[:, Z_U:Z_CQ]
    mq_ref[...] = z[:, Z_MQ:Z_MK].astype(BF16)
    mk_ref[...] = (z[:, Z_MK:Z_MV] * 0.125).astype(BF16)
    mv_ref[...] = (z[:, Z_MV:Z_MO] + vonem_ref[...]).astype(BF16)
    mo_ref[...] = z[:, Z_MO:Z_TAIL]

    tail = z[:, Z_TAIL:Z_END]
    rot = _rope_lanes(tail, tcos_ref[...], tslo_ref[...], tshi_ref[...])
    kr_ref[...] = rot[:, :MLA_ROPE]

    gt = tail + bcol_ref[...]
    lane = lax.broadcasted_iota(jnp.int32, gt.shape, 1)
    gcol_ref[...] = jnp.where((lane >= TAIL_LF) & (lane < TAIL_LF + 4), _log_sigmoid(gt), gt)

    gr = _dot_nt(wgt_ref[...], hb) + brow_ref[...]
    row = lax.broadcasted_iota(jnp.int32, gr.shape, 0)
    grow_ref[...] = jnp.where(row >= 4, _log_sigmoid(gr), gr)

    kT_ref[...] = (_dot_nt(wkt_ref[...], hb) * 0.125).astype(BF16)

    cqn = _rms(z[:, Z_CQ:Z_CKV], qn_ref[...]).astype(BF16)
    qa = _dot_nt(wuqt_ref[...], cqn) * scale
    qcos, qsin = qcos_ref[...], qsin_ref[...]
    r0, r1, r2 = MLA_NOPE, MLA_NOPE + ROT, MLA_NOPE + 2 * ROT
    for h in range(n_heads):
        g = qa[h * HEAD_PAD:(h + 1) * HEAD_PAD, :]
        sh = jnp.concatenate([g[0:r0], g[r1:r2], g[r0:r1], g[r2:HEAD_PAD]], axis=0)
        qT_ref[h] = (g * qcos + sh * qsin).astype(BF16)

    ckvn = _rms(z[:, Z_CKV:Z_MQ], kvn_ref[...])
    ckv_ref[...] = ckvn
    cb = ckvn.astype(BF16)
    kp = (_dot(cb, wkt_top_ref[...]) + _dot(rot.astype(BF16), wkt_bot_ref[...])).astype(BF16)
    vt = (_dot_nt(wvt_ref[...], cb) + vone_ref[...]).astype(BF16)
    for h in range(n_heads):
        kpad_ref[h] = kp[:, h * HEAD_PAD:(h + 1) * HEAD_PAD]
        for c in range(vT_ref.shape[1]):
            vT_ref[h, c] = vt[h * HEAD_PAD:(h + 1) * HEAD_PAD, c * tk:(c + 1) * tk]


def _inproj(x, wl, tables, *, tm, tk, tiles_per_seq):
    n, d = x.shape
    nh = wl['n_heads']
    grid = (n // tm,)
    row = lambda w: pl.BlockSpec((tm, w), lambda i: (i, 0))
    tab = pl.BlockSpec((tm, HEAD_PAD), lambda i: (i % tiles_per_seq, 0))
    tab_t = pl.BlockSpec((HEAD_PAD, tm), lambda i: (0, i % tiles_per_seq))
    weights = [wl['g_pre'], wl['wz'], wl['wgt'], wl['wkt'], wl['bcol'], wl['brow']]
    mla_w = [wl['qn'], wl['wuqt'], wl['kvn'], wl['wk_top'], wl['wk_bot'], wl['wvt'], wl['vone'], wl['vonem']]
    in_specs = ([row(d)] + [_const_spec(w.shape) for w in weights] + [tab] * 3 + [tab_t] * 2
                + [_const_spec(w.shape) for w in mla_w])
    out_shape = [
        jax.ShapeDtypeStruct((n, 256), F32),
        jax.ShapeDtypeStruct((nh, HEAD_PAD, n), BF16),
        jax.ShapeDtypeStruct((n, 256), F32),
        jax.ShapeDtypeStruct((n, MLA_ROPE), F32),
        jax.ShapeDtypeStruct((nh, n, HEAD_PAD), BF16),
        jax.ShapeDtypeStruct((nh, n // tk, HEAD_PAD, tk), BF16),
        jax.ShapeDtypeStruct((n, 512), BF16),
        jax.ShapeDtypeStruct((n, 512), BF16),
        jax.ShapeDtypeStruct((n, 512), BF16),
        jax.ShapeDtypeStruct((n, 256), F32),
        jax.ShapeDtypeStruct((n, HEAD_PAD), F32),
        jax.ShapeDtypeStruct((8, n), F32),
        jax.ShapeDtypeStruct((256, n), BF16),
    ]
    out_specs = [row(256), pl.BlockSpec((nh, HEAD_PAD, tm), lambda i: (0, 0, i)), row(256), row(MLA_ROPE),
                 pl.BlockSpec((nh, tm, HEAD_PAD), lambda i: (0, i, 0)),
                 pl.BlockSpec((nh, tm // tk, HEAD_PAD, tk), lambda i: (0, i, 0, 0)),
                 row(512), row(512), row(512), row(256), row(HEAD_PAD),
                 pl.BlockSpec((8, tm), lambda i: (0, i)), pl.BlockSpec((256, tm), lambda i: (0, i))]
    scale = (MLA_NOPE + MLA_ROPE) ** -0.5 * math.log2(math.e)
    kern = functools.partial(_inproj_kernel, n_heads=nh, scale=scale, tk=tk)
    return pl.pallas_call(kern, out_shape=out_shape, grid=grid, in_specs=in_specs, out_specs=out_specs,
                          compiler_params=_params(("parallel",)), name="inproj")(
        x, *weights, *tables, *mla_w)


def _ssm_discretise(are, aim, ldt):
    dt = jnp.exp(ldt)
    mag = jnp.exp(are * dt)
    lbr = mag * jnp.cos(aim * dt)
    lbi = mag * jnp.sin(aim * dt)
    inv = 1.0 / (are * are + aim * aim)
    fr = ((lbr - 1.0) * are + lbi * aim) * inv
    fi = (lbi * are - (lbr - 1.0) * aim) * inv
    return lbr, lbi, fr, fi


def _ssm_input_matrix(fr, fi, bre, bim):
    return jnp.concatenate([fr * bre - fi * bim, fr * bim + fi * bre], axis=1).astype(BF16)


def _ssm_tail(xcat, u, ccat_ref, d_ref, wglu_ref, bglu_ref, gout_ref):
    y = _dot(xcat.astype(BF16), ccat_ref[...]) + d_ref[...] * u
    y = jax.nn.gelu(y)
    y = y * jax.nn.sigmoid(_dot(y.astype(BF16), wglu_ref[...]) + bglu_ref[...])
    return _rms(y, gout_ref[...])


def _cmul(ar, ai, br, bi):
    return ar * br - ai * bi, ar * bi + ai * br


def _ssm_prompt_kernel(u_ref, are_ref, aim_ref, ldt_ref, bre_ref, bim_ref, ccat_ref, d_ref, wglu_ref, bglu_ref,
                       gout_ref, y_ref, st_ref, x_scr, car_scr, *, tc, ns):
    c = pl.program_id(1)
    lbr, lbi, fr, fi = _ssm_discretise(are_ref[...], aim_ref[...], ldt_ref[...])
    u = u_ref[...]
    x_scr[...] = _dot(u.astype(BF16), _ssm_input_matrix(fr, fi, bre_ref[...], bim_ref[...]))

    @pl.when(c == 0)
    def _():
        car_scr[...] = jnp.zeros_like(car_scr)

    p = [(lbr, lbi)]
    for _ in range(7):
        p.append(_cmul(p[-1][0], p[-1][1], lbr, lbi))
    rowi = lax.broadcasted_iota(jnp.int32, (8, ns), 0)
    pwr = jnp.zeros((8, ns), F32)
    pwi = jnp.zeros((8, ns), F32)
    for j in range(8):
        pwr = jnp.where(rowi == j, p[j][0], pwr)
        pwi = jnp.where(rowi == j, p[j][1], pwi)
    coefs = []
    for dd in (1, 2, 4):
        coefs.append((dd, jnp.where(rowi >= dd, p[dd - 1][0], 0.0), jnp.where(rowi >= dd, p[dd - 1][1], 0.0)))

    def body(g, carry):
        hr, hi = carry
        r0 = pl.multiple_of(g * 8, 8)
        xr = x_scr[pl.ds(r0, 8), 0:ns]
        xi = x_scr[pl.ds(r0, 8), ns:2 * ns]
        for dd, ar, ai in coefs:
            sr = pltpu.roll(xr, dd, 0)
            si = pltpu.roll(xi, dd, 0)
            xr, xi = xr + ar * sr - ai * si, xi + ar * si + ai * sr
        xr = xr + pwr * hr - pwi * hi
        xi = xi + pwr * hi + pwi * hr
        x_scr[pl.ds(r0, 8), 0:ns] = xr
        x_scr[pl.ds(r0, 8), ns:2 * ns] = xi
        return xr[7:8, :], xi[7:8, :]

    hr, hi = lax.fori_loop(0, tc // 8, body, (car_scr[0:1, 0:ns], car_scr[0:1, ns:2 * ns]))
    car_scr[0:1, 0:ns] = hr
    car_scr[0:1, ns:2 * ns] = hi
    st_ref[0:1, 0:ns] = hr
    st_ref[0:1, ns:2 * ns] = hi
    y_ref[...] = _ssm_tail(x_scr[...], u, ccat_ref, d_ref, wglu_ref, bglu_ref, gout_ref)


def _ssm_prompt(u, wl, *, batch, seq, tc):
    n, w = u.shape
    ns = wl['ssm_are'].shape[1]
    nt = seq // tc
    weights = [wl['ssm_are'], wl['ssm_aim'], wl['ssm_ldt'], wl['ssm_bre'], wl['ssm_bim'], wl['ssm_ccat'],
               wl['ssm_d'], wl['ssm_wglu'], wl['ssm_bglu'], wl['g_ssm']]
    kern = functools.partial(_ssm_prompt_kernel, tc=tc, ns=ns)
    y, st = pl.pallas_call(
        kern,
        out_shape=[jax.ShapeDtypeStruct((n, w), F32), jax.ShapeDtypeStruct((batch, 1, 2 * ns), F32)],
        grid=(batch, nt),
        in_specs=[pl.BlockSpec((tc, w), lambda b, c: (b * nt + c, 0))] + [_const_spec(a.shape) for a in weights],
        out_specs=[pl.BlockSpec((tc, w), lambda b, c: (b * nt + c, 0)),
                   pl.BlockSpec((None, 1, 2 * ns), lambda b, c: (b, 0, 0))],
        scratch_shapes=[pltpu.VMEM((tc, 2 * ns), F32), pltpu.VMEM((8, 2 * ns), F32)],
        compiler_params=_params(("parallel", "arbitrary")), name="ssm_prompt")(u, *weights)
    return y, st[:, 0, :ns], st[:, 0, ns:]


def _ssm_decode_kernel(u_ref, h0r_ref, h0i_ref, are_ref, aim_ref, ldt_ref, bre_ref, bim_ref, ccat_ref, d_ref,
                       wglu_ref, bglu_ref, gout_ref, y_ref, hr_ref, hi_ref, *, ns):
    lbr, lbi, fr, fi = _ssm_discretise(are_ref[...], aim_ref[...], ldt_ref[...])
    u = u_ref[...]
    bu = _dot(u.astype(BF16), _ssm_input_matrix(fr, fi, bre_ref[...], bim_ref[...]))
    h0r, h0i = h0r_ref[...], h0i_ref[...]
    hr = bu[:, 0:ns] + lbr * h0r - lbi * h0i
    hi = bu[:, ns:2 * ns] + lbr * h0i + lbi * h0r
    hr_ref[...] = hr
    hi_ref[...] = hi
    y_ref[...] = _ssm_tail(jnp.concatenate([hr, hi], axis=1), u, ccat_ref, d_ref, wglu_ref, bglu_ref, gout_ref)


def _ssm_decode(u, h0r, h0i, wl):
    n, w = u.shape
    ns = wl['ssm_are'].shape[1]
    weights = [wl['ssm_are'], wl['ssm_aim'], wl['ssm_ldt'], wl['ssm_bre'], wl['ssm_bim'], wl['ssm_ccat'],
               wl['ssm_d'], wl['ssm_wglu'], wl['ssm_bglu'], wl['g_ssm']]
    kern = functools.partial(_ssm_decode_kernel, ns=ns)
    return pl.pallas_call(
        kern,
        out_shape=[jax.ShapeDtypeStruct((n, w), F32), jax.ShapeDtypeStruct((n, ns), F32),
                   jax.ShapeDtypeStruct((n, ns), F32)],
        compiler_params=pltpu.CompilerParams(vmem_limit_bytes=VMEM_LIMIT), name="ssm_decode")(u, h0r, h0i, *weights)


def _attn_kernel(qT_ref, k_ref, vT_ref, o_ref, m_scr, acc_scr, *, n_heads, tq, dv):
    qi = pl.program_id(1)
    keyi = lax.broadcasted_iota(jnp.int32, (tq, tq), 0)
    qryi = lax.broadcasted_iota(jnp.int32, (tq, tq), 1)
    causal = keyi <= qryi
    d0 = pl.multiple_of(qi * tq, tq)
    ss = [_dot(k_ref[h, pl.ds(d0, tq), :], qT_ref[h]) for h in range(n_heads)]
    ps = []
    for h in range(n_heads):
        s = jnp.where(causal, ss[h], -jnp.inf)
        m = jnp.max(s, axis=0, keepdims=True)
        m_scr[h] = m
        ps.append(jnp.exp2(s - m).astype(BF16))
    for h in range(n_heads):
        acc_scr[h] = _dot(vT_ref[h, qi], ps[h])

    def body(j, _):
        k0 = pl.multiple_of(j * tq, tq)
        ss = [_dot(k_ref[h, pl.ds(k0, tq), :], qT_ref[h]) for h in range(n_heads)]
        ps, alphas = [], []
        for h in range(n_heads):
            m_old = m_scr[h]
            m_new = jnp.maximum(m_old, jnp.max(ss[h], axis=0, keepdims=True))
            ps.append(jnp.exp2(ss[h] - m_new).astype(BF16))
            alphas.append(jnp.exp2(m_old - m_new))
            m_scr[h] = m_new
        for h in range(n_heads):
            acc_scr[h] = alphas[h] * acc_scr[h] + _dot(vT_ref[h, j], ps[h])
        return 0

    lax.fori_loop(0, qi, body, 0)
    outs = []
    for h in range(n_heads):
        acc = acc_scr[h]
        outs.append(acc[:dv, :] / acc[dv:dv + 1, :])
    o_ref[...] = jnp.concatenate(outs, axis=0).T


def _attn_prompt(qT, k, vT, *, batch, seq, n_heads, tq, dv):
    n = k.shape[1]
    nq = seq // tq
    kern = functools.partial(_attn_kernel, n_heads=n_heads, tq=tq, dv=dv)
    return pl.pallas_call(
        kern,
        out_shape=jax.ShapeDtypeStruct((n, n_heads * dv), F32),
        grid=(batch, nq),
        in_specs=[pl.BlockSpec((n_heads, HEAD_PAD, tq), lambda b, i: (0, 0, b * nq + i)),
                  pl.BlockSpec((n_heads, seq, HEAD_PAD), lambda b, i: (0, b, 0)),
                  pl.BlockSpec((n_heads, nq, HEAD_PAD, tq), lambda b, i: (0, b, 0, 0))],
        out_specs=pl.BlockSpec((tq, n_heads * dv), lambda b, i: (b * nq + i, 0)),
        scratch_shapes=[pltpu.VMEM((n_heads, 1, tq), F32), pltpu.VMEM((n_heads, HEAD_PAD, tq), F32)],
        compiler_params=_params(("parallel", "arbitrary")), name="attn_prompt")(qT, k, vT)


def _qlat_kernel(qT_ref, wuk_ref, qlat_ref, qrope_ref, *, n_heads):
    for h in range(n_heads):
        qlat_ref[h] = _dot(wuk_ref[h], qT_ref[h, 0:MLA_NOPE, :]).astype(BF16)
        qrope_ref[h] = qT_ref[h, MLA_NOPE:MLA_NOPE + MLA_ROPE, :]


def _qlat(qT, wl):
    nh, _, n = qT.shape
    r = wl['wuk_h'].shape[1]
    kern = functools.partial(_qlat_kernel, n_heads=nh)
    return pl.pallas_call(
        kern,
        out_shape=[jax.ShapeDtypeStruct((nh, r, n), BF16), jax.ShapeDtypeStruct((nh, MLA_ROPE, n), BF16)],
        compiler_params=pltpu.CompilerParams(vmem_limit_bytes=VMEM_LIMIT), name="qlat")(qT, wl['wuk_h'])


def _dattn_kernel(pt_ref, qlat_ref, qrope_ref, cnew_ref, knew_ref, ckv_hbm, krt_hbm, o_ref,
                  ckv_buf, krt_buf, sem, m_scr, l_scr, acc_scr, *, layer, n_pages, pps):
    s_id, c = pl.program_id(0), pl.program_id(1)
    nc = pl.num_programs(1)
    step = s_id * nc + c
    slot = step % 2

    def page_copies(slot_, s_, c_):
        out = []
        for j in range(pps):
            page = pt_ref[s_ * n_pages + c_ * pps + j]
            out.append(pltpu.make_async_copy(ckv_hbm.at[layer, page], ckv_buf.at[slot_, j], sem.at[slot_]))
            out.append(pltpu.make_async_copy(krt_hbm.at[layer, page], krt_buf.at[slot_, j], sem.at[slot_]))
        return out

    @pl.when(step == 0)
    def _():
        for cp in page_copies(slot, s_id, c):
            cp.start()

    @pl.when(step + 1 < pl.num_programs(0) * nc)
    def _():
        wrap = c + 1 == nc
        for cp in page_copies(1 - slot, jnp.where(wrap, s_id + 1, s_id), jnp.where(wrap, 0, c + 1)):
            cp.start()

    ql = qlat_ref[...]
    qr = qrope_ref[...]

    @pl.when(c == 0)
    def _():
        cn = cnew_ref[...].astype(BF16).astype(F32)
        kn = knew_ref[...].astype(BF16).astype(F32)
        s_self = (jnp.sum(ql.astype(F32) * cn, axis=1, keepdims=True)
                  + jnp.sum(qr.astype(F32) * kn, axis=1, keepdims=True))
        m_scr[...] = s_self
        l_scr[...] = jnp.ones_like(l_scr)
        acc_scr[...] = jnp.broadcast_to(cn, acc_scr.shape)

    for cp in page_copies(slot, s_id, c):
        cp.wait()

    pages = [ckv_buf[slot, j].astype(BF16) for j in range(pps)]
    s = jnp.concatenate(
        [_dot_nt(ql, pages[j]) + _dot(qr, krt_buf[slot, j].astype(BF16)) for j in range(pps)], axis=1)
    m_old = m_scr[...]
    m_new = jnp.maximum(m_old, jnp.max(s, axis=1, keepdims=True))
    alpha = jnp.exp2(m_old - m_new)
    p = jnp.exp2(s - m_new)
    l_scr[...] = alpha * l_scr[...] + jnp.sum(p, axis=1, keepdims=True)
    pb = p.astype(BF16)
    page = pages[0].shape[0]
    pv = _dot(pb[:, 0:page], pages[0])
    for j in range(1, pps):
        pv = pv + _dot(pb[:, j * page:(j + 1) * page], pages[j])
    acc_scr[...] = alpha * acc_scr[...] + pv
    m_scr[...] = m_new

    @pl.when(c == nc - 1)
    def _():
        o_ref[...] = acc_scr[...] / l_scr[...]


def _dattn(qlat, qrope, cnew, knew, cache_ckv, cache_krt, pt_flat, *, layer, n_pages, pps):
    nb, nh, r = qlat.shape
    page = cache_ckv.shape[2]
    nc = n_pages // pps
    per_seq = lambda shape: pl.BlockSpec((None,) + shape, lambda s, c, pt: (s, 0, 0))
    hbm = pl.BlockSpec(memory_space=pl.ANY)
    grid_spec = pltpu.PrefetchScalarGridSpec(
        num_scalar_prefetch=1, grid=(nb, nc),
        in_specs=[per_seq((nh, r)), per_seq((nh, MLA_ROPE)), per_seq((1, r)), per_seq((1, MLA_ROPE)), hbm, hbm],
        out_specs=per_seq((nh, r)),
        scratch_shapes=[pltpu.VMEM((2, pps, page, r), F32), pltpu.VMEM((2, pps, MLA_ROPE, page), F32),
                        pltpu.SemaphoreType.DMA((2,)),
                        pltpu.VMEM((nh, 1), F32), pltpu.VMEM((nh, 1), F32), pltpu.VMEM((nh, r), F32)])
    kern = functools.partial(_dattn_kernel, layer=layer, n_pages=n_pages, pps=pps)
    return pl.pallas_call(
        kern, out_shape=jax.ShapeDtypeStruct((nb, nh, r), F32), grid_spec=grid_spec,
        compiler_params=_params(("arbitrary", "arbitrary")), name="attn_decode")(
        pt_flat, qlat, qrope, cnew, knew, cache_ckv, cache_krt)


def _uv_kernel(o_ref, wuv_ref, y_ref, *, n_heads):
    y_ref[...] = jnp.concatenate([_dot(o_ref[h].astype(BF16), wuv_ref[h]) for h in range(n_heads)], axis=1)


def _uv(olat_hm, wl):
    nh, n, _ = olat_hm.shape
    dv = wl['wuv_h'].shape[2]
    kern = functools.partial(_uv_kernel, n_heads=nh)
    return pl.pallas_call(kern, out_shape=jax.ShapeDtypeStruct((n, nh * dv), F32),
                          compiler_params=pltpu.CompilerParams(vmem_limit_bytes=VMEM_LIMIT), name="uv")(
        olat_hm, wl['wuv_h'])


def _split3(x):
    hi = x.astype(BF16)
    r1 = x - hi.astype(F32)
    mid = r1.astype(BF16)
    lo = (r1 - mid.astype(F32)).astype(BF16)
    return hi, mid, lo


def _mlstm_prompt_kernel(q_ref, k_ref, kT_ref, v_ref, mo_ref, gcol_ref, grow_ref, gout_ref,
                         y_ref, cn_ref, m_ref, cn_scr, m_scr, *, n_heads, tl, dh):
    c = pl.program_id(1)

    @pl.when(c == 0)
    def _():
        cn_scr[...] = jnp.zeros_like(cn_scr)
        m_scr[...] = jnp.zeros_like(m_scr)

    rowi = lax.broadcasted_iota(jnp.int32, (tl, tl), 0)
    coli = lax.broadcasted_iota(jnp.int32, (tl, tl), 1)
    causal = coli <= rowi
    tril = causal.astype(BF16)
    triu = (rowi <= coli).astype(BF16)
    gcol = gcol_ref[...]
    grow = grow_ref[...]
    bcol = sum(_dot(tril, part) for part in _split3(gcol))
    brow = sum(_dot(part, triu) for part in _split3(grow))
    outs = []
    for h in range(n_heads):
        hs = slice(h * HEAD_PAD, (h + 1) * HEAD_PAD)
        qh, kh, vh = q_ref[:, hs], k_ref[:, hs], v_ref[:, hs]
        bc = bcol[:, TAIL_LF + h:TAIL_LF + h + 1]
        br = brow[4 + h:5 + h, :]
        ir = grow[h:h + 1, :]
        m_prev = m_scr[h:h + 1, 0:1]
        lw = jnp.where(causal, bc - (br - ir), -jnp.inf)
        m_inter = bc + m_prev
        mt = jnp.maximum(m_inter, jnp.max(lw, axis=1, keepdims=True))
        w = jnp.exp(lw - mt)
        g = jnp.exp(m_inter - mt)
        qk = _dot_nt(qh, kh) * w
        cn = cn_scr[h]
        tot = g * _dot(qh[:, :dh], cn.astype(BF16)) + _dot(qk.astype(BF16), vh)
        den = tot[:, dh:dh + 1]
        outs.append(tot[:, :dh] / jnp.maximum(jnp.abs(den), jnp.exp(-mt)))
        w_end = w[tl - 1:tl, :]
        g_end = g[tl - 1:tl, :]
        kw = (kT_ref[h * dh:(h + 1) * dh, :].astype(F32) * w_end).astype(BF16)
        cn_scr[h] = g_end * cn + _dot(kw, vh)
        m_scr[h:h + 1, :] = jnp.broadcast_to(mt[tl - 1:tl, :], (1, m_scr.shape[1]))
    hcat = jnp.concatenate(outs, axis=1)
    y_ref[...] = _rms(hcat * jax.nn.sigmoid(mo_ref[...]), gout_ref[...])
    cn_ref[...] = cn_scr[...]
    m_ref[...] = m_scr[...]


def _mlstm_prompt(mq, mk, kT, mv, mo, gcol, grow, wl, *, batch, seq, tl):
    n = mq.shape[0]
    nh = wl['m_heads']
    dh = mo.shape[1] // nh
    nt = seq // tl
    tok = lambda w: pl.BlockSpec((tl, w), lambda b, c: (b * nt + c, 0))
    kern = functools.partial(_mlstm_prompt_kernel, n_heads=nh, tl=tl, dh=dh)
    y, cn, m = pl.pallas_call(
        kern,
        out_shape=[jax.ShapeDtypeStruct((n, nh * dh), F32),
                   jax.ShapeDtypeStruct((batch, nh, dh, HEAD_PAD), F32),
                   jax.ShapeDtypeStruct((batch, 8, HEAD_PAD), F32)],
        grid=(batch, nt),
        in_specs=[tok(nh * HEAD_PAD), tok(nh * HEAD_PAD),
                  pl.BlockSpec((nh * dh, tl), lambda b, c: (0, b * nt + c)),
                  tok(nh * HEAD_PAD), tok(nh * dh), tok(HEAD_PAD),
                  pl.BlockSpec((8, tl), lambda b, c: (0, b * nt + c)),
                  _const_spec(wl['g_mlstm'].shape)],
        out_specs=[tok(nh * dh),
                   pl.BlockSpec((None, nh, dh, HEAD_PAD), lambda b, c: (b, 0, 0, 0)),
                   pl.BlockSpec((None, 8, HEAD_PAD), lambda b, c: (b, 0, 0))],
        scratch_shapes=[pltpu.VMEM((nh, dh, HEAD_PAD), F32), pltpu.VMEM((8, HEAD_PAD), F32)],
        compiler_params=_params(("parallel", "arbitrary")), name="mlstm_prompt")(
        mq, mk, kT, mv, mo, gcol, grow, wl['g_mlstm'])
    return y, cn[..., :dh], cn[..., dh], m[:, :nh, 0]


def _mlstm_decode_kernel(q_ref, k_ref, v_ref, mo_ref, gcol_ref, c0_ref, n0_ref, m0_ref, gout_ref,
                         y_ref, c_ref, n_ref, m_ref, q_scr, k_scr, v_scr, h_scr, *, n_heads, tb, dh):
    q_scr[...] = q_ref[...].astype(F32)
    k_scr[...] = k_ref[...].astype(F32)
    v_scr[...] = v_ref[...].astype(F32)
    eye = (lax.broadcasted_iota(jnp.int32, (dh, dh), 0) == lax.broadcasted_iota(jnp.int32, (dh, dh), 1)).astype(BF16)
    lane = lax.broadcasted_iota(jnp.int32, (1, HEAD_PAD), 1)

    def body(i, _):
        gates = gcol_ref[pl.ds(i, 1), :]
        m0row = m0_ref[pl.ds(i, 1), :]
        qrow = q_scr[pl.ds(i, 1), :]
        krow = k_scr[pl.ds(i, 1), :]
        vrow = v_scr[pl.ds(i, 1), :]
        mrow = jnp.zeros((1, HEAD_PAD), F32)
        hs = []
        for h in range(n_heads):
            qh = qrow[:, h * HEAD_PAD:h * HEAD_PAD + dh]
            kh = krow[:, h * HEAD_PAD:h * HEAD_PAD + dh]
            vh = vrow[:, h * HEAD_PAD:h * HEAD_PAD + dh]
            ig = gates[:, TAIL_IG + h:TAIL_IG + h + 1]
            lf = gates[:, TAIL_LF + h:TAIL_LF + h + 1]
            m_inter = lf + m0row[:, h:h + 1]
            mt = jnp.maximum(m_inter, ig)
            w = jnp.exp(ig - mt)
            g = jnp.exp(m_inter - mt)
            qk = jnp.sum(qh * kh, axis=1, keepdims=True) * w
            c0 = c0_ref[i, h]
            n0 = n0_ref[i, pl.ds(h, 1), :]
            qc = _dot(jnp.broadcast_to(qh, (8, dh)).astype(BF16), c0.astype(BF16))[0:1, :]
            qn = jnp.sum(qh * n0.astype(BF16).astype(F32), axis=1, keepdims=True)
            num = g * qc + qk * vh
            den = g * qn + qk
            hs.append(num / jnp.maximum(jnp.abs(den), jnp.exp(-mt)))
            kcol = _dot_nt(eye, jnp.broadcast_to(kh, (8, dh)).astype(BF16))[:, 0:1]
            c_ref[i, h] = g * c0 + (w * kcol) * vh
            n_ref[i, pl.ds(h, 1), :] = g * n0 + w * kh
            mrow = jnp.where(lane == h, mt, mrow)
        h_scr[pl.ds(i, 1), :] = jnp.concatenate(hs, axis=1)
        m_ref[pl.ds(i, 1), :] = mrow
        return 0

    lax.fori_loop(0, tb, body, 0)
    y_ref[...] = _rms(h_scr[...] * jax.nn.sigmoid(mo_ref[...]), gout_ref[...])


def _mlstm_decode(mq, mk, mv, mo, gcol, c0, n0, m0, wl, *, tb):
    n = mq.shape[0]
    nh = wl['m_heads']
    dh = mo.shape[1] // nh
    m0p = jnp.pad(m0, ((0, 0), (0, HEAD_PAD - nh)))
    tok = lambda w: pl.BlockSpec((tb, w), lambda i: (i, 0))
    kern = functools.partial(_mlstm_decode_kernel, n_heads=nh, tb=tb, dh=dh)
    y, c, nn, m = pl.pallas_call(
        kern,
        out_shape=[jax.ShapeDtypeStruct((n, nh * dh), F32), jax.ShapeDtypeStruct(c0.shape, F32),
                   jax.ShapeDtypeStruct(n0.shape, F32), jax.ShapeDtypeStruct((n, HEAD_PAD), F32)],
        grid=(n // tb,),
        in_specs=[tok(nh * HEAD_PAD), tok(nh * HEAD_PAD), tok(nh * HEAD_PAD), tok(nh * dh), tok(HEAD_PAD),
                  pl.BlockSpec((tb, nh, dh, dh), lambda i: (i, 0, 0, 0)),
                  pl.BlockSpec((tb, nh, dh), lambda i: (i, 0, 0)), tok(HEAD_PAD),
                  _const_spec(wl['g_mlstm'].shape)],
        out_specs=[tok(nh * dh), pl.BlockSpec((tb, nh, dh, dh), lambda i: (i, 0, 0, 0)),
                   pl.BlockSpec((tb, nh, dh), lambda i: (i, 0, 0)), tok(HEAD_PAD)],
        scratch_shapes=[pltpu.VMEM((tb, nh * HEAD_PAD), F32)] * 3 + [pltpu.VMEM((tb, nh * dh), F32)],
        compiler_params=_params(("parallel",)), name="mlstm_decode")(
        mq, mk, mv, mo, gcol, c0, n0, m0p, wl['g_mlstm'])
    return y, c, nn, m[:, :nh]


def _outffn_kernel(x_ref, ya_ref, yb_ref, yc_ref, gmla_ref, wout_ref, gpost_ref, gfpre_ref, wg_ref, wu_ref, wd_ref,
                   gfpost_ref, o_ref, *, ff_chunk):
    yb = _rms(yb_ref[...], gmla_ref[...])
    y = jnp.concatenate([ya_ref[...], yb, yc_ref[...]], axis=1).astype(BF16)
    x1 = x_ref[...] + _rms(_dot(y, wout_ref[...]), gpost_ref[...])
    h2 = _rms(x1, gfpre_ref[...]).astype(BF16)
    dff = wg_ref.shape[1]
    f = None
    for c0 in range(0, dff, ff_chunk):
        a = jax.nn.silu(_dot(h2, wg_ref[:, c0:c0 + ff_chunk])) * _dot(h2, wu_ref[:, c0:c0 + ff_chunk])
        part = _dot(a.astype(BF16), wd_ref[c0:c0 + ff_chunk, :])
        f = part if f is None else f + part
    o_ref[...] = x1 + _rms(f, gfpost_ref[...])


def _outffn(x, ya, yb, yc, wl, *, tm):
    n, d = x.shape
    weights = [wl['g_mla'], wl['wout'], wl['g_post'], wl['g_fpre'], wl['wg'], wl['wu'], wl['wd'], wl['g_fpost']]
    row = lambda w: pl.BlockSpec((tm, w), lambda i: (i, 0))
    dff = wl['wg'].shape[1]
    kern = functools.partial(_outffn_kernel, ff_chunk=dff // 2)
    return pl.pallas_call(
        kern, out_shape=jax.ShapeDtypeStruct((n, d), F32), grid=(n // tm,),
        in_specs=[row(d), row(ya.shape[1]), row(yb.shape[1]), row(yc.shape[1])]
        + [_const_spec(w.shape) for w in weights],
        out_specs=row(d), compiler_params=_params(("parallel",)), name="outffn")(x, ya, yb, yc, *weights)


def _rope_tables(positions):
    pos = np.asarray(positions, dtype=np.float64)[:, None]
    inv = ROPE_THETA ** (-np.arange(ROT, dtype=np.float64) / ROT)
    cos, sin = np.cos(pos * inv), np.sin(pos * inv)
    n = pos.shape[0]

    c = np.zeros((n, HEAD_PAD))
    lo = np.zeros((n, HEAD_PAD))
    hi = np.zeros((n, HEAD_PAD))
    c[:, 0:ROT] = cos
    c[:, ROT:2 * ROT] = cos
    lo[:, 0:ROT] = -sin
    hi[:, ROT:2 * ROT] = sin
    qc = np.zeros((HEAD_PAD, n))
    qs = np.zeros((HEAD_PAD, n))
    qc[:MLA_NOPE] = 1.0
    qc[MLA_NOPE:MLA_NOPE + ROT] = cos.T
    qc[MLA_NOPE + ROT:MLA_NOPE + 2 * ROT] = cos.T
    qs[MLA_NOPE:MLA_NOPE + ROT] = -sin.T
    qs[MLA_NOPE + ROT:MLA_NOPE + 2 * ROT] = sin.T
    return [jnp.asarray(a, dtype=F32) for a in (c, lo, hi, qc, qs)]


def _pad_heads(w, n_heads):
    rows = w.shape[0]
    w3 = w.reshape(rows, n_heads, -1)
    return jnp.pad(w3, ((0, 0), (0, 0), (0, HEAD_PAD - w3.shape[2]))).reshape(rows, n_heads * HEAD_PAD)


def _prep_layer(l, p):
    f = lambda name: p[name][l]
    row = lambda a: a.reshape(1, -1).astype(F32)
    w_in = f('w_in')
    d = w_in.shape[0]
    nh = p['mla_w_uk'].shape[2]
    mh = p['mlstm_b_i'].shape[1]
    o = 0
    cols = {}
    for name, width in (('u', 256), ('cq', 384), ('ckv', 256), ('kr', MLA_ROPE), ('mq', 256), ('mk', 256),
                        ('mv', 256), ('mo', 256), ('mi', mh), ('mf', mh)):
        cols[name] = w_in[:, o:o + width]
        o += width
    tail = jnp.concatenate([cols['kr'], cols['mi'], cols['mf'],
                            jnp.zeros((d, HEAD_PAD - MLA_ROPE - 2 * mh), F32)], axis=1)
    wz = jnp.concatenate([cols['u'], cols['cq'], cols['ckv'], _pad_heads(cols['mq'], mh), _pad_heads(cols['mk'], mh),
                          _pad_heads(cols['mv'], mh), cols['mo'], tail], axis=1).astype(BF16)
    bcol = jnp.zeros((1, HEAD_PAD), F32).at[0, TAIL_IG:TAIL_IG + mh].set(f('mlstm_b_i'))
    bcol = bcol.at[0, TAIL_LF:TAIL_LF + mh].set(f('mlstm_b_f'))
    brow = jnp.concatenate([f('mlstm_b_i'), f('mlstm_b_f')]).reshape(2 * mh, 1)
    one_lane = lambda n: jnp.zeros((n, HEAD_PAD), F32).at[:, MLA_NOPE].set(1.0).reshape(1, n * HEAD_PAD)

    w_uk = f('mla_w_uk')
    w_uv = f('mla_w_uv')
    r = w_uk.shape[0]
    wk_top = jnp.pad(w_uk, ((0, 0), (0, 0), (0, HEAD_PAD - MLA_NOPE))).reshape(r, nh * HEAD_PAD)
    place = jnp.zeros((HEAD_PAD, nh, HEAD_PAD), F32)
    e = jnp.arange(MLA_ROPE)
    place = place.at[e, :, MLA_NOPE + e].set(1.0)

    g_ = p['ssm_a_re'].shape[1]
    eye_g = jnp.eye(g_, dtype=F32)
    bre = jnp.einsum('gpc,gh->gchp', f('ssm_b_re'), eye_g).reshape(256, -1)
    bim = jnp.einsum('gpc,gh->gchp', f('ssm_b_im'), eye_g).reshape(256, -1)
    cre = jnp.einsum('gcp,gh->gphc', f('ssm_c_re'), eye_g).reshape(-1, 256)
    cim = jnp.einsum('gcp,gh->gphc', f('ssm_c_im'), eye_g).reshape(-1, 256)
    ns = bre.shape[1]
    return dict(
        n_heads=nh, m_heads=mh,
        g_pre=row(f('norm_mix_pre')), wz=wz,
        wgt=jnp.concatenate([cols['mi'], cols['mf']], axis=1).T.astype(BF16),
        wkt=cols['mk'].T.astype(BF16), bcol=bcol, brow=brow,
        qn=row(f('mla_q_norm')), wuqt=_pad_heads(f('mla_w_uq'), nh).T.astype(BF16), kvn=row(f('mla_kv_norm')),
        wk_top=wk_top.astype(BF16), wk_bot=place.reshape(HEAD_PAD, nh * HEAD_PAD).astype(BF16),
        wvt=jnp.pad(w_uv, ((0, 0), (0, 0), (0, HEAD_PAD - w_uv.shape[2]))).reshape(r, nh * HEAD_PAD).T.astype(BF16),
        vone=one_lane(nh).reshape(nh * HEAD_PAD, 1), vonem=one_lane(mh),
        wuk_h=w_uk.transpose(1, 0, 2).astype(BF16),
        wuv_h=w_uv.transpose(1, 0, 2).astype(BF16),
        ssm_are=row(f('ssm_a_re')), ssm_aim=row(f('ssm_a_im')),
        ssm_ldt=jnp.repeat(f('ssm_log_dt'), ns // g_).reshape(1, ns),
        ssm_bre=bre, ssm_bim=bim, ssm_ccat=jnp.concatenate([cre, -cim], axis=0).astype(BF16),
        ssm_d=row(f('ssm_d')), ssm_wglu=f('ssm_w_glu').astype(BF16), ssm_bglu=row(f('ssm_b_glu')),
        g_ssm=row(f('out_norm_ssm')), g_mla=row(f('out_norm_mla')), g_mlstm=row(f('out_norm_mlstm')),
        wout=f('w_out').astype(BF16), g_post=row(f('norm_mix_post')), g_fpre=row(f('norm_ffn_pre')),
        wg=f('ffn_w_gate').astype(BF16), wu=f('ffn_w_up').astype(BF16), wd=f('ffn_w_down').astype(BF16),
        g_fpost=row(f('norm_ffn_post')))


def kernel(x_prompt, x_sample, cache_ckv, cache_krope, page_table, state_ssm_re, state_ssm_im, state_mlstm_C, state_mlstm_n, state_mlstm_m, norm_mix_pre, norm_mix_post, norm_ffn_pre, norm_ffn_post, w_in, ssm_a_re, ssm_a_im, ssm_b_re, ssm_b_im, ssm_c_re, ssm_c_im, ssm_d, ssm_log_dt, ssm_w_glu, ssm_b_glu, mla_q_norm, mla_w_uq, mla_kv_norm, mla_w_uk, mla_w_uv, mlstm_b_i, mlstm_b_f, out_norm_ssm, out_norm_mla, out_norm_mlstm, w_out, ffn_w_gate, ffn_w_up, ffn_w_down):
    p = dict(norm_mix_pre=norm_mix_pre, norm_mix_post=norm_mix_post, norm_ffn_pre=norm_ffn_pre,
             norm_ffn_post=norm_ffn_post, w_in=w_in, ssm_a_re=ssm_a_re, ssm_a_im=ssm_a_im, ssm_b_re=ssm_b_re,
             ssm_b_im=ssm_b_im, ssm_c_re=ssm_c_re, ssm_c_im=ssm_c_im, ssm_d=ssm_d, ssm_log_dt=ssm_log_dt,
             ssm_w_glu=ssm_w_glu, ssm_b_glu=ssm_b_glu, mla_q_norm=mla_q_norm, mla_w_uq=mla_w_uq,
             mla_kv_norm=mla_kv_norm, mla_w_uk=mla_w_uk, mla_w_uv=mla_w_uv, mlstm_b_i=mlstm_b_i,
             mlstm_b_f=mlstm_b_f, out_norm_ssm=out_norm_ssm, out_norm_mla=out_norm_mla,
             out_norm_mlstm=out_norm_mlstm, w_out=w_out, ffn_w_gate=ffn_w_gate, ffn_w_up=ffn_w_up,
             ffn_w_down=ffn_w_down)
    depth = w_in.shape[0]
    bsz, seq, d = x_prompt.shape
    db, dec_t, _ = x_sample.shape
    assert dec_t == 1, "the sample group advances one token per sequence"
    n_pages, page = page_table.shape[1], cache_ckv.shape[2]
    past = n_pages * page
    g_, ns_g = ssm_a_re.shape[1], ssm_a_re.shape[2]
    ns = g_ * ns_g

    tm_p = min(TM_PROJ, seq)
    tm_f = min(TM_FFN, seq)
    tc = min(T_SSM, seq)
    tq = min(T_ATT, seq)
    tl = min(T_MLSTM, seq)
    tb = min(TOK_MLSTM_DEC, db)
    pps = min(PAGES_PER_STEP, n_pages)
    assert seq % tm_p == 0 and seq % tm_f == 0 and seq % tc == 0 and seq % tq == 0 and seq % tl == 0
    assert db % tb == 0 and n_pages % pps == 0

    tabs_p = _rope_tables(np.arange(seq))
    tabs_s = _rope_tables(np.full((db,), past))
    pt_flat = page_table.reshape(-1).astype(jnp.int32)
    cache_krt = jnp.swapaxes(cache_krope, 2, 3)

    xp = x_prompt.reshape(bsz * seq, d)
    xs = x_sample.reshape(db, d)
    outs_p, outs_s = [], []
    for l in range(depth):
        wl = _prep_layer(l, p)
        nh, mh = wl['n_heads'], wl['m_heads']
        dv = wl['wuv_h'].shape[2]

        (u, qT, ckvn, kr, kpad, vT, mq, mk, mv, mo, gcol, grow, kT) = _inproj(
            xp, wl, tabs_p, tm=tm_p, tk=tq, tiles_per_seq=seq // tm_p)
        ya, s_re, s_im = _ssm_prompt(u, wl, batch=bsz, seq=seq, tc=tc)
        yb = _attn_prompt(qT, kpad, vT, batch=bsz, seq=seq, n_heads=nh, tq=tq, dv=dv)
        yc, c_p, n_p, m_p = _mlstm_prompt(mq, mk, kT, mv, mo, gcol, grow, wl, batch=bsz, seq=seq, tl=tl)
        xp = _outffn(xp, ya, yb, yc, wl, tm=tm_f)
        outs_p.append((ckvn.reshape(bsz, seq, -1), kr.reshape(bsz, seq, -1),
                       s_re.reshape(bsz, g_, ns_g), s_im.reshape(bsz, g_, ns_g), c_p, n_p, m_p))

        (u, qT, ckvn, kr, _, _, mq, mk, mv, mo, gcol, _, _) = _inproj(
            xs, wl, tabs_s, tm=db, tk=db, tiles_per_seq=1)
        ya, h_re, h_im = _ssm_decode(u, state_ssm_re[l].reshape(db, ns), state_ssm_im[l].reshape(db, ns), wl)
        qlat, qrope = _qlat(qT, wl)
        olat = _dattn(qlat.transpose(2, 0, 1), qrope.transpose(2, 0, 1), ckvn.reshape(db, 1, -1),
                      kr.reshape(db, 1, -1), cache_ckv, cache_krt, pt_flat, layer=l, n_pages=n_pages, pps=pps)
        yb = _uv(olat.transpose(1, 0, 2), wl)
        yc, c_s, n_s, m_s = _mlstm_decode(mq, mk, mv, mo, gcol, state_mlstm_C[l], state_mlstm_n[l],
                                          state_mlstm_m[l], wl, tb=tb)
        xs = _outffn(xs, ya, yb, yc, wl, tm=db)
        outs_s.append((ckvn.reshape(db, 1, -1), kr.reshape(db, 1, -1),
                       h_re.reshape(db, g_, ns_g), h_im.reshape(db, g_, ns_g), c_s, n_s, m_s))

    stk = lambda lst, i: jnp.stack([s[i] for s in lst], axis=0)
    return (xp.reshape(bsz, seq, d), xs.reshape(db, 1, d),
            stk(outs_p, 0), stk(outs_p, 1), stk(outs_s, 0), stk(outs_s, 1),
            stk(outs_p, 2), stk(outs_p, 3), stk(outs_s, 2), stk(outs_s, 3),
            stk(outs_p, 4), stk(outs_p, 5), stk(outs_p, 6),
            stk(outs_s, 4), stk(outs_s, 5), stk(outs_s, 6))
```

```python
import functools
import math

import numpy as np
import jax
import jax.numpy as jnp
from jax import lax
from jax.experimental import pallas as pl
from jax.experimental.pallas import tpu as pltpu

F32 = jnp.float32
BF16 = jnp.bfloat16

RMS_EPS = 1e-6
ROPE_THETA = 10000.0
MLA_NOPE = 64
MLA_ROPE = 32
LANES = 128
HEAD_PAD = LANES
ROT = MLA_ROPE // 2

TM_PROJ = 512
TM_FFN = 512
T_SSM = 512
T_ATT = 256
T_MLSTM = 256
TOK_MLSTM_DEC = 16
PAGES_PER_STEP = 32
VMEM_LIMIT = 56 * 1024 * 1024

NT_DIMS = (((1,), (1,)), ((), ()))


def _dot(a, b):
    return jnp.dot(a, b, preferred_element_type=F32)


def _dot_nt(a, b):
    return lax.dot_general(a, b, NT_DIMS, preferred_element_type=F32)


def _rms(x, g):
    return x * lax.rsqrt(jnp.mean(x * x, axis=-1, keepdims=True) + RMS_EPS) * g


def _log_sigmoid(x):
    return jnp.minimum(x, 0.0) - jnp.log1p(jnp.exp(-jnp.abs(x)))


def _rope_lanes(c, cos_t, sin_lo, sin_hi):
    return c * cos_t + pltpu.roll(c, HEAD_PAD - ROT, 1) * sin_lo + pltpu.roll(c, ROT, 1) * sin_hi


def _const_spec(shape):
    nd = len(shape)
    return pl.BlockSpec(shape, lambda *_: (0,) * nd, pipeline_mode=pl.Buffered(1))


def _params(sem):
    return pltpu.CompilerParams(dimension_semantics=sem, vmem_limit_bytes=VMEM_LIMIT)


Z_U, Z_CQ, Z_CKV, Z_MQ, Z_MK, Z_MV, Z_MO, Z_TAIL, Z_END = 0, 256, 640, 896, 1408, 1920, 2432, 2688, 2816
TAIL_IG = MLA_ROPE
TAIL_LF = MLA_ROPE + 4


def _inproj_kernel(x_ref, gpre_ref, wz_ref, wgt_ref, wkt_ref, bcol_ref, brow_ref,
                   tcos_ref, tslo_ref, tshi_ref, qcos_ref, qsin_ref,
                   qn_ref, wuqt_ref, kvn_ref, wkt_top_ref, wkt_bot_ref, wvt_ref, vone_ref, vonem_ref,
                   u_ref, qT_ref, ckv_ref, kr_ref, kpad_ref, vT_ref,
                   mq_ref, mk_ref, mv_ref, mo_ref, gcol_ref, grow_ref, kT_ref, *, n_heads, scale, tk):
    hb = _rms(x_ref[...], gpre_ref[...]).astype(BF16)
    z = _dot(hb, wz_ref[...])

    u_ref[...] = z[:, Z_U:Z_CQ]
    mq_ref[...] = z[:, Z_MQ:Z_MK].astype(BF16)
    mk_ref[...] = (z[:, Z_MK:Z_MV] * 0.125).astype(BF16)
    mv_ref[...] = (z[:, Z_MV:Z_MO] + vonem_ref[...]).astype(BF16)
    mo_ref[...] = z[:, Z_MO:Z_TAIL]

    tail = z[:, Z_TAIL:Z_END]
    rot = _rope_lanes(tail, tcos_ref[...], tslo_ref[...], tshi_ref[...])
    kr_ref[...] = rot[:, :MLA_ROPE]

    gt = tail + bcol_ref[...]
    lane = lax.broadcasted_iota(jnp.int32, gt.shape, 1)
    gcol_ref[...] = jnp.where((lane >= TAIL_LF) & (lane < TAIL_LF + 4), _log_sigmoid(gt), gt)

    gr = _dot_nt(wgt_ref[...], hb) + brow_ref[...]
    row = lax.broadcasted_iota(jnp.int32, gr.shape, 0)
    grow_ref[...] = jnp.where(row >= 4, _log_sigmoid(gr), gr)

    kT_ref[...] = (_dot_nt(wkt_ref[...], hb) * 0.125).astype(BF16)

    cqn = _rms(z[:, Z_CQ:Z_CKV], qn_ref[...]).astype(BF16)
    qa = _dot_nt(wuqt_ref[...], cqn) * scale
    qcos, qsin = qcos_ref[...], qsin_ref[...]
    r0, r1, r2 = MLA_NOPE, MLA_NOPE + ROT, MLA_NOPE + 2 * ROT
    for h in range(n_heads):
        g = qa[h * HEAD_PAD:(h + 1) * HEAD_PAD, :]
        sh = jnp.concatenate([g[0:r0], g[r1:r2], g[r0:r1], g[r2:HEAD_PAD]], axis=0)
        qT_ref[h] = (g * qcos + sh * qsin).astype(BF16)

    ckvn = _rms(z[:, Z_CKV:Z_MQ], kvn_ref[...])
    ckv_ref[...] = ckvn
    cb = ckvn.astype(BF16)
    kp = (_dot(cb, wkt_top_ref[...]) + _dot(rot.astype(BF16), wkt_bot_ref[...])).astype(BF16)
    vt = (_dot_nt(wvt_ref[...], cb) + vone_ref[...]).astype(BF16)
    for h in range(n_heads):
        kpad_ref[h] = kp[:, h * HEAD_PAD:(h + 1) * HEAD_PAD]
        for c in range(vT_ref.shape[1]):
            vT_ref[h, c] = vt[h * HEAD_PAD:(h + 1) * HEAD_PAD, c * tk:(c + 1) * tk]


def _inproj(x, wl, tables, *, tm, tk, tiles_per_seq):
    n, d = x.shape
    nh = wl['n_heads']
    grid = (n // tm,)
    row = lambda w: pl.BlockSpec((tm, w), lambda i: (i, 0))
    tab = pl.BlockSpec((tm, HEAD_PAD), lambda i: (i % tiles_per_seq, 0))
    tab_t = pl.BlockSpec((HEAD_PAD, tm), lambda i: (0, i % tiles_per_seq))
    weights = [wl['g_pre'], wl['wz'], wl['wgt'], wl['wkt'], wl['bcol'], wl['brow']]
    mla_w = [wl['qn'], wl['wuqt'], wl['kvn'], wl['wk_top'], wl['wk_bot'], wl['wvt'], wl['vone'], wl['vonem']]
    in_specs = ([row(d)] + [_const_spec(w.shape) for w in weights] + [tab] * 3 + [tab_t] * 2
                + [_const_spec(w.shape) for w in mla_w])
    out_shape = [
        jax.ShapeDtypeStruct((n, 256), F32),
        jax.ShapeDtypeStruct((nh, HEAD_PAD, n), BF16),
        jax.ShapeDtypeStruct((n, 256), F32),
        jax.ShapeDtypeStruct((n, MLA_ROPE), F32),
        jax.ShapeDtypeStruct((nh, n, HEAD_PAD), BF16),
        jax.ShapeDtypeStruct((nh, n // tk, HEAD_PAD, tk), BF16),
        jax.ShapeDtypeStruct((n, 512), BF16),
        jax.ShapeDtypeStruct((n, 512), BF16),
        jax.ShapeDtypeStruct((n, 512), BF16),
        jax.ShapeDtypeStruct((n, 256), F32),
        jax.ShapeDtypeStruct((n, HEAD_PAD), F32),
        jax.ShapeDtypeStruct((8, n), F32),
        jax.ShapeDtypeStruct((256, n), BF16),
    ]
    out_specs = [row(256), pl.BlockSpec((nh, HEAD_PAD, tm), lambda i: (0, 0, i)), row(256), row(MLA_ROPE),
                 pl.BlockSpec((nh, tm, HEAD_PAD), lambda i: (0, i, 0)),
                 pl.BlockSpec((nh, tm // tk, HEAD_PAD, tk), lambda i: (0, i, 0, 0)),
                 row(512), row(512), row(512), row(256), row(HEAD_PAD),
                 pl.BlockSpec((8, tm), lambda i: (0, i)), pl.BlockSpec((256, tm), lambda i: (0, i))]
    scale = (MLA_NOPE + MLA_ROPE) ** -0.5 * math.log2(math.e)
    kern = functools.partial(_inproj_kernel, n_heads=nh, scale=scale, tk=tk)
    return pl.pallas_call(kern, out_shape=out_shape, grid=grid, in_specs=in_specs, out_specs=out_specs,
                          compiler_params=_params(("parallel",)), name="inproj")(
        x, *weights, *tables, *mla_w)


def _ssm_discretise(are, aim, ldt):
    dt = jnp.exp(ldt)
    mag = jnp.exp(are * dt)
    lbr = mag * jnp.cos(aim * dt)
    lbi = mag * jnp.sin(aim * dt)
    inv = 1.0 / (are * are + aim * aim)
    fr = ((lbr - 1.0) * are + lbi * aim) * inv
    fi = (lbi * are - (lbr - 1.0) * aim) * inv
    return lbr, lbi, fr, fi


def _ssm_input_matrix(fr, fi, bre, bim):
    return jnp.concatenate([fr * bre - fi * bim, fr * bim + fi * bre], axis=1).astype(BF16)


def _ssm_tail(xcat, u, ccat_ref, d_ref, wglu_ref, bglu_ref, gout_ref):
    y = _dot(xcat.astype(BF16), ccat_ref[...]) + d_ref[...] * u
    y = jax.nn.gelu(y)
    y = y * jax.nn.sigmoid(_dot(y.astype(BF16), wglu_ref[...]) + bglu_ref[...])
    return _rms(y, gout_ref[...])


def _cmul(ar, ai, br, bi):
    return ar * br - ai * bi, ar * bi + ai * br


def _ssm_prompt_kernel(u_ref, are_ref, aim_ref, ldt_ref, bre_ref, bim_ref, ccat_ref, d_ref, wglu_ref, bglu_ref,
                       gout_ref, y_ref, st_ref, x_scr, car_scr, bb_scr, coef_scr, *, tc, ns):
    c = pl.program_id(1)

    @pl.when(c == 0)
    def _():
        car_scr[...] = jnp.zeros_like(car_scr)
        lbr, lbi, fr, fi = _ssm_discretise(are_ref[...], aim_ref[...], ldt_ref[...])
        bb_scr[...] = _ssm_input_matrix(fr, fi, bre_ref[...], bim_ref[...])
        p = [(lbr, lbi)]
        for _ in range(7):
            p.append(_cmul(p[-1][0], p[-1][1], lbr, lbi))
        rowi = lax.broadcasted_iota(jnp.int32, (8, ns), 0)
        pwr = jnp.zeros((8, ns), F32)
        pwi = jnp.zeros((8, ns), F32)
        for j in range(8):
            pwr = jnp.where(rowi == j, p[j][0], pwr)
            pwi = jnp.where(rowi == j, p[j][1], pwi)
        coef_scr[0] = pwr
        coef_scr[1] = pwi
        for i, dd in enumerate((1, 2, 4)):
            coef_scr[2 + 2 * i] = jnp.where(rowi >= dd, p[dd - 1][0], 0.0)
            coef_scr[3 + 2 * i] = jnp.where(rowi >= dd, p[dd - 1][1], 0.0)

    u = u_ref[...]
    x_scr[...] = _dot(u.astype(BF16), bb_scr[...])
    pwr, pwi = coef_scr[0], coef_scr[1]
    coefs = [(dd, coef_scr[2 + 2 * i], coef_scr[3 + 2 * i]) for i, dd in enumerate((1, 2, 4))]

    def body(g, carry):
        hr, hi = carry
        r0 = pl.multiple_of(g * 8, 8)
        xr = x_scr[pl.ds(r0, 8), 0:ns]
        xi = x_scr[pl.ds(r0, 8), ns:2 * ns]
        for dd, ar, ai in coefs:
            sr = pltpu.roll(xr, dd, 0)
            si = pltpu.roll(xi, dd, 0)
            xr, xi = xr + ar * sr - ai * si, xi + ar * si + ai * sr
        xr = xr + pwr * hr - pwi * hi
        xi = xi + pwr * hi + pwi * hr
        x_scr[pl.ds(r0, 8), 0:ns] = xr
        x_scr[pl.ds(r0, 8), ns:2 * ns] = xi
        return xr[7:8, :], xi[7:8, :]

    hr, hi = lax.fori_loop(0, tc // 8, body, (car_scr[0:1, 0:ns], car_scr[0:1, ns:2 * ns]))
    car_scr[0:1, 0:ns] = hr
    car_scr[0:1, ns:2 * ns] = hi
    st_ref[0:1, 0:ns] = hr
    st_ref[0:1, ns:2 * ns] = hi
    y_ref[...] = _ssm_tail(x_scr[...], u, ccat_ref, d_ref, wglu_ref, bglu_ref, gout_ref)


def _ssm_prompt(u, wl, *, batch, seq, tc):
    n, w = u.shape
    ns = wl['ssm_are'].shape[1]
    nt = seq // tc
    weights = [wl['ssm_are'], wl['ssm_aim'], wl['ssm_ldt'], wl['ssm_bre'], wl['ssm_bim'], wl['ssm_ccat'],
               wl['ssm_d'], wl['ssm_wglu'], wl['ssm_bglu'], wl['g_ssm']]
    kern = functools.partial(_ssm_prompt_kernel, tc=tc, ns=ns)
    y, st = pl.pallas_call(
        kern,
        out_shape=[jax.ShapeDtypeStruct((n, w), F32), jax.ShapeDtypeStruct((batch, 1, 2 * ns), F32)],
        grid=(batch, nt),
        in_specs=[pl.BlockSpec((tc, w), lambda b, c: (b * nt + c, 0))] + [_const_spec(a.shape) for a in weights],
        out_specs=[pl.BlockSpec((tc, w), lambda b, c: (b * nt + c, 0)),
                   pl.BlockSpec((None, 1, 2 * ns), lambda b, c: (b, 0, 0))],
        scratch_shapes=[pltpu.VMEM((tc, 2 * ns), F32), pltpu.VMEM((8, 2 * ns), F32),
                        pltpu.VMEM((w, 2 * ns), BF16), pltpu.VMEM((8, 8, ns), F32)],
        compiler_params=_params(("parallel", "arbitrary")), name="ssm_prompt")(u, *weights)
    return y, st[:, 0, :ns], st[:, 0, ns:]


def _ssm_decode_kernel(u_ref, h0r_ref, h0i_ref, are_ref, aim_ref, ldt_ref, bre_ref, bim_ref, ccat_ref, d_ref,
                       wglu_ref, bglu_ref, gout_ref, y_ref, hr_ref, hi_ref, *, ns):
    lbr, lbi, fr, fi = _ssm_discretise(are_ref[...], aim_ref[...], ldt_ref[...])
    u = u_ref[...]
    bu = _dot(u.astype(BF16), _ssm_input_matrix(fr, fi, bre_ref[...], bim_ref[...]))
    h0r, h0i = h0r_ref[...], h0i_ref[...]
    hr = bu[:, 0:ns] + lbr * h0r - lbi * h0i
    hi = bu[:, ns:2 * ns] + lbr * h0i + lbi * h0r
    hr_ref[...] = hr
    hi_ref[...] = hi
    y_ref[...] = _ssm_tail(jnp.concatenate([hr, hi], axis=1), u, ccat_ref, d_ref, wglu_ref, bglu_ref, gout_ref)


def _ssm_decode(u, h0r, h0i, wl):
    n, w = u.shape
    ns = wl['ssm_are'].shape[1]
    weights = [wl['ssm_are'], wl['ssm_aim'], wl['ssm_ldt'], wl['ssm_bre'], wl['ssm_bim'], wl['ssm_ccat'],
               wl['ssm_d'], wl['ssm_wglu'], wl['ssm_bglu'], wl['g_ssm']]
    kern = functools.partial(_ssm_decode_kernel, ns=ns)
    return pl.pallas_call(
        kern,
        out_shape=[jax.ShapeDtypeStruct((n, w), F32), jax.ShapeDtypeStruct((n, ns), F32),
                   jax.ShapeDtypeStruct((n, ns), F32)],
        compiler_params=pltpu.CompilerParams(vmem_limit_bytes=VMEM_LIMIT), name="ssm_decode")(u, h0r, h0i, *weights)


def _attn_kernel(qT_ref, k_ref, vT_ref, o_ref, m_scr, acc_scr, sa_scr, sb_scr, *, n_heads, tq, dv):
    qi = pl.program_id(1)
    keyi = lax.broadcasted_iota(jnp.int32, (tq, tq), 0)
    qryi = lax.broadcasted_iota(jnp.int32, (tq, tq), 1)
    causal = keyi <= qryi
    heads = range(n_heads)

    def scores(j, dst):
        k0 = pl.multiple_of(j * tq, tq)
        for h in heads:
            dst[h] = _dot(k_ref[h, pl.ds(k0, tq), :], qT_ref[h])

    def softmax_values(j, src, mask):
        ps, alphas = [], []
        for h in heads:
            s = src[h]
            if mask:
                s = jnp.where(causal, s, -jnp.inf)
            m_old = m_scr[h]
            m_new = jnp.maximum(m_old, jnp.max(s, axis=0, keepdims=True))
            ps.append(jnp.exp2(s - m_new).astype(BF16))
            alphas.append(jnp.exp2(m_old - m_new))
            m_scr[h] = m_new
        for h in heads:
            acc_scr[h] = alphas[h] * acc_scr[h] + _dot(vT_ref[h, j], ps[h])

    m_scr[...] = jnp.full(m_scr.shape, -jnp.inf, F32)
    acc_scr[...] = jnp.zeros_like(acc_scr)
    scores(0, sa_scr)

    def body(t, _):
        j = 2 * t
        scores(j + 1, sb_scr)
        softmax_values(j, sa_scr, False)
        scores(j + 2, sa_scr)
        softmax_values(j + 1, sb_scr, False)
        return 0

    lax.fori_loop(0, qi // 2, body, 0)

    @pl.when(qi % 2 == 0)
    def _():
        softmax_values(qi, sa_scr, True)

    @pl.when(qi % 2 == 1)
    def _():
        scores(qi, sb_scr)
        softmax_values(qi - 1, sa_scr, False)
        softmax_values(qi, sb_scr, True)

    outs = []
    for h in heads:
        acc = acc_scr[h]
        outs.append(acc[:dv, :] / acc[dv:dv + 1, :])
    o_ref[...] = jnp.concatenate(outs, axis=0).T


def _attn_prompt(qT, k, vT, *, batch, seq, n_heads, tq, dv):
    n = k.shape[1]
    nq = seq // tq
    kern = functools.partial(_attn_kernel, n_heads=n_heads, tq=tq, dv=dv)
    return pl.pallas_call(
        kern,
        out_shape=jax.ShapeDtypeStruct((n, n_heads * dv), F32),
        grid=(batch, nq),
        in_specs=[pl.BlockSpec((n_heads, HEAD_PAD, tq), lambda b, i: (0, 0, b * nq + i)),
                  pl.BlockSpec((n_heads, seq, HEAD_PAD), lambda b, i: (0, b, 0)),
                  pl.BlockSpec((n_heads, nq, HEAD_PAD, tq), lambda b, i: (0, b, 0, 0))],
        out_specs=pl.BlockSpec((tq, n_heads * dv), lambda b, i: (b * nq + i, 0)),
        scratch_shapes=[pltpu.VMEM((n_heads, 1, tq), F32), pltpu.VMEM((n_heads, HEAD_PAD, tq), F32),
                        pltpu.VMEM((n_heads, tq, tq), F32), pltpu.VMEM((n_heads, tq, tq), F32)],
        compiler_params=_params(("parallel", "arbitrary")), name="attn_prompt")(qT, k, vT)


def _qlat_kernel(qT_ref, wuk_ref, qlat_ref, qrope_ref, *, n_heads):
    for h in range(n_heads):
        qlat_ref[h] = _dot(wuk_ref[h], qT_ref[h, 0:MLA_NOPE, :]).astype(BF16)
        qrope_ref[h] = qT_ref[h, MLA_NOPE:MLA_NOPE + MLA_ROPE, :]


def _qlat(qT, wl):
    nh, _, n = qT.shape
    r = wl['wuk_h'].shape[1]
    kern = functools.partial(_qlat_kernel, n_heads=nh)
    return pl.pallas_call(
        kern,
        out_shape=[jax.ShapeDtypeStruct((nh, r, n), BF16), jax.ShapeDtypeStruct((nh, MLA_ROPE, n), BF16)],
        compiler_params=pltpu.CompilerParams(vmem_limit_bytes=VMEM_LIMIT), name="qlat")(qT, wl['wuk_h'])


def _dattn_kernel(pt_ref, qlat_ref, qrope_ref, cnew_ref, knew_ref, ckv_hbm, krt_hbm, o_ref,
                  ckv_buf, krt_buf, sem, m_scr, l_scr, acc_scr, *, layer, n_pages, pps):
    s_id, c = pl.program_id(0), pl.program_id(1)
    nc = pl.num_programs(1)
    step = s_id * nc + c
    slot = step % 2

    def page_copies(slot_, s_, c_):
        out = []
        for j in range(pps):
            page = pt_ref[s_ * n_pages + c_ * pps + j]
            out.append(pltpu.make_async_copy(ckv_hbm.at[layer, page], ckv_buf.at[slot_, j], sem.at[slot_]))
            out.append(pltpu.make_async_copy(krt_hbm.at[layer, page], krt_buf.at[slot_, j], sem.at[slot_]))
        return out

    @pl.when(step == 0)
    def _():
        for cp in page_copies(slot, s_id, c):
            cp.start()

    @pl.when(step + 1 < pl.num_programs(0) * nc)
    def _():
        wrap = c + 1 == nc
        for cp in page_copies(1 - slot, jnp.where(wrap, s_id + 1, s_id), jnp.where(wrap, 0, c + 1)):
            cp.start()

    ql = qlat_ref[...]
    qr = qrope_ref[...]

    @pl.when(c == 0)
    def _():
        cn = cnew_ref[...].astype(BF16).astype(F32)
        kn = knew_ref[...].astype(BF16).astype(F32)
        s_self = (jnp.sum(ql.astype(F32) * cn, axis=1, keepdims=True)
                  + jnp.sum(qr.astype(F32) * kn, axis=1, keepdims=True))
        m_scr[...] = s_self
        l_scr[...] = jnp.ones_like(l_scr)
        acc_scr[...] = jnp.broadcast_to(cn, acc_scr.shape)

    for cp in page_copies(slot, s_id, c):
        cp.wait()

    page = ckv_buf.shape[2]
    pages = [ckv_buf[slot, j].astype(BF16) for j in range(pps)]
    s = jnp.concatenate(
        [_dot_nt(ql, pages[j]) + _dot(qr, krt_buf[slot, j].astype(BF16)) for j in range(pps)], axis=1)
    m_old = m_scr[...]
    m_new = jnp.maximum(m_old, jnp.max(s, axis=1, keepdims=True))
    alpha = jnp.exp2(m_old - m_new)
    p = jnp.exp2(s - m_new)
    l_scr[...] = alpha * l_scr[...] + jnp.sum(p, axis=1, keepdims=True)
    pb = p.astype(BF16)
    pv = _dot(pb[:, 0:page], pages[0])
    for j in range(1, pps):
        pv = pv + _dot(pb[:, j * page:(j + 1) * page], pages[j])
    acc_scr[...] = alpha * acc_scr[...] + pv
    m_scr[...] = m_new

    @pl.when(c == nc - 1)
    def _():
        o_ref[...] = acc_scr[...] / l_scr[...]


def _dattn(qlat, qrope, cnew, knew, cache_ckv, cache_krt, pt_flat, *, layer, n_pages, pps):
    nb, nh, r = qlat.shape
    page = cache_ckv.shape[2]
    nc = n_pages // pps
    per_seq = lambda shape: pl.BlockSpec((None,) + shape, lambda s, c, pt: (s, 0, 0))
    hbm = pl.BlockSpec(memory_space=pl.ANY)
    grid_spec = pltpu.PrefetchScalarGridSpec(
        num_scalar_prefetch=1, grid=(nb, nc),
        in_specs=[per_seq((nh, r)), per_seq((nh, MLA_ROPE)), per_seq((1, r)), per_seq((1, MLA_ROPE)), hbm, hbm],
        out_specs=per_seq((nh, r)),
        scratch_shapes=[pltpu.VMEM((2, pps, page, r), F32), pltpu.VMEM((2, pps, MLA_ROPE, page), F32),
                        pltpu.SemaphoreType.DMA((2,)),
                        pltpu.VMEM((nh, 1), F32), pltpu.VMEM((nh, 1), F32), pltpu.VMEM((nh, r), F32)])
    kern = functools.partial(_dattn_kernel, layer=layer, n_pages=n_pages, pps=pps)
    return pl.pallas_call(
        kern, out_shape=jax.ShapeDtypeStruct((nb, nh, r), F32), grid_spec=grid_spec,
        compiler_params=_params(("arbitrary", "arbitrary")), name="attn_decode")(
        pt_flat, qlat, qrope, cnew, knew, cache_ckv, cache_krt)


def _uv_kernel(o_ref, wuv_ref, y_ref, *, n_heads):
    y_ref[...] = jnp.concatenate([_dot(o_ref[h].astype(BF16), wuv_ref[h]) for h in range(n_heads)], axis=1)


def _uv(olat_hm, wl):
    nh, n, _ = olat_hm.shape
    dv = wl['wuv_h'].shape[2]
    kern = functools.partial(_uv_kernel, n_heads=nh)
    return pl.pallas_call(kern, out_shape=jax.ShapeDtypeStruct((n, nh * dv), F32),
                          compiler_params=pltpu.CompilerParams(vmem_limit_bytes=VMEM_LIMIT), name="uv")(
        olat_hm, wl['wuv_h'])


def _split3(x):
    hi = x.astype(BF16)
    r1 = x - hi.astype(F32)
    mid = r1.astype(BF16)
    lo = (r1 - mid.astype(F32)).astype(BF16)
    return hi, mid, lo


def _mlstm_prompt_kernel(q_ref, k_ref, kT_ref, v_ref, mo_ref, gcol_ref, grow_ref, gout_ref,
                         y_ref, cn_ref, m_ref, cn_scr, m_scr, *, n_heads, tl, dh):
    c = pl.program_id(1)

    @pl.when(c == 0)
    def _():
        cn_scr[...] = jnp.zeros_like(cn_scr)
        m_scr[...] = jnp.zeros_like(m_scr)

    rowi = lax.broadcasted_iota(jnp.int32, (tl, tl), 0)
    coli = lax.broadcasted_iota(jnp.int32, (tl, tl), 1)
    causal = coli <= rowi
    tril = causal.astype(BF16)
    triu = (rowi <= coli).astype(BF16)
    gcol = gcol_ref[...]
    grow = grow_ref[...]
    bcol = sum(_dot(tril, part) for part in _split3(gcol))
    brow = sum(_dot(part, triu) for part in _split3(grow))
    heads = range(n_heads)
    hs = [slice(h * HEAD_PAD, (h + 1) * HEAD_PAD) for h in heads]
    cns = [cn_scr[h] for h in heads]
    s_qk = [_dot_nt(q_ref[:, hs[h]], k_ref[:, hs[h]]) for h in heads]
    q_c = [_dot(q_ref[:, h * HEAD_PAD:h * HEAD_PAD + dh], cns[h].astype(BF16)) for h in heads]
    qks, kws, gs, mts = [], [], [], []
    for h in heads:
        bc = bcol[:, TAIL_LF + h:TAIL_LF + h + 1]
        br = brow[4 + h:5 + h, :]
        ir = grow[h:h + 1, :]
        m_prev = m_scr[h:h + 1, 0:1]
        lw = jnp.where(causal, bc - (br - ir), -jnp.inf)
        m_inter = bc + m_prev
        mt = jnp.maximum(m_inter, jnp.max(lw, axis=1, keepdims=True))
        w = jnp.exp(lw - mt)
        gs.append(jnp.exp(m_inter - mt))
        mts.append(mt)
        qks.append((s_qk[h] * w).astype(BF16))
        kws.append((kT_ref[h * dh:(h + 1) * dh, :].astype(F32) * w[tl - 1:tl, :]).astype(BF16))
    qkv = [_dot(qks[h], v_ref[:, hs[h]]) for h in heads]
    kwv = [_dot(kws[h], v_ref[:, hs[h]]) for h in heads]
    outs = []
    for h in heads:
        tot = gs[h] * q_c[h] + qkv[h]
        den = tot[:, dh:dh + 1]
        outs.append(tot[:, :dh] / jnp.maximum(jnp.abs(den), jnp.exp(-mts[h])))
        cn_scr[h] = gs[h][tl - 1:tl, :] * cns[h] + kwv[h]
        m_scr[h:h + 1, :] = jnp.broadcast_to(mts[h][tl - 1:tl, :], (1, m_scr.shape[1]))
    hcat = jnp.concatenate(outs, axis=1)
    y_ref[...] = _rms(hcat * jax.nn.sigmoid(mo_ref[...]), gout_ref[...])
    cn_ref[...] = cn_scr[...]
    m_ref[...] = m_scr[...]


def _mlstm_prompt(mq, mk, kT, mv, mo, gcol, grow, wl, *, batch, seq, tl):
    n = mq.shape[0]
    nh = wl['m_heads']
    dh = mo.shape[1] // nh
    nt = seq // tl
    tok = lambda w: pl.BlockSpec((tl, w), lambda b, c: (b * nt + c, 0))
    kern = functools.partial(_mlstm_prompt_kernel, n_heads=nh, tl=tl, dh=dh)
    y, cn, m = pl.pallas_call(
        kern,
        out_shape=[jax.ShapeDtypeStruct((n, nh * dh), F32),
                   jax.ShapeDtypeStruct((batch, nh, dh, HEAD_PAD), F32),
                   jax.ShapeDtypeStruct((batch, 8, HEAD_PAD), F32)],
        grid=(batch, nt),
        in_specs=[tok(nh * HEAD_PAD), tok(nh * HEAD_PAD),
                  pl.BlockSpec((nh * dh, tl), lambda b, c: (0, b * nt + c)),
                  tok(nh * HEAD_PAD), tok(nh * dh), tok(HEAD_PAD),
                  pl.BlockSpec((8, tl), lambda b, c: (0, b * nt + c)),
                  _const_spec(wl['g_mlstm'].shape)],
        out_specs=[tok(nh * dh),
                   pl.BlockSpec((None, nh, dh, HEAD_PAD), lambda b, c: (b, 0, 0, 0)),
                   pl.BlockSpec((None, 8, HEAD_PAD), lambda b, c: (b, 0, 0))],
        scratch_shapes=[pltpu.VMEM((nh, dh, HEAD_PAD), F32), pltpu.VMEM((8, HEAD_PAD), F32)],
        compiler_params=_params(("parallel", "arbitrary")), name="mlstm_prompt")(
        mq, mk, kT, mv, mo, gcol, grow, wl['g_mlstm'])
    return y, cn[..., :dh], cn[..., dh], m[:, :nh, 0]


def _mlstm_decode_kernel(q_ref, k_ref, v_ref, mo_ref, gcol_ref, c0_ref, n0_ref, m0_ref, gout_ref,
                         y_ref, c_ref, n_ref, m_ref, q_scr, k_scr, v_scr, h_scr, *, n_heads, tb, dh):
    q_scr[...] = q_ref[...].astype(F32)
    k_scr[...] = k_ref[...].astype(F32)
    v_scr[...] = v_ref[...].astype(F32)
    eye = (lax.broadcasted_iota(jnp.int32, (dh, dh), 0) == lax.broadcasted_iota(jnp.int32, (dh, dh), 1)).astype(BF16)
    lane = lax.broadcasted_iota(jnp.int32, (1, HEAD_PAD), 1)

    def body(i, _):
        gates = gcol_ref[pl.ds(i, 1), :]
        m0row = m0_ref[pl.ds(i, 1), :]
        qrow = q_scr[pl.ds(i, 1), :]
        krow = k_scr[pl.ds(i, 1), :]
        vrow = v_scr[pl.ds(i, 1), :]
        mrow = jnp.zeros((1, HEAD_PAD), F32)
        hs = []
        for h in range(n_heads):
            qh = qrow[:, h * HEAD_PAD:h * HEAD_PAD + dh]
            kh = krow[:, h * HEAD_PAD:h * HEAD_PAD + dh]
            vh = vrow[:, h * HEAD_PAD:h * HEAD_PAD + dh]
            ig = gates[:, TAIL_IG + h:TAIL_IG + h + 1]
            lf = gates[:, TAIL_LF + h:TAIL_LF + h + 1]
            m_inter = lf + m0row[:, h:h + 1]
            mt = jnp.maximum(m_inter, ig)
            w = jnp.exp(ig - mt)
            g = jnp.exp(m_inter - mt)
            qk = jnp.sum(qh * kh, axis=1, keepdims=True) * w
            c0 = c0_ref[i, h]
            n0 = n0_ref[i, pl.ds(h, 1), :]
            qc = _dot(jnp.broadcast_to(qh, (8, dh)).astype(BF16), c0.astype(BF16))[0:1, :]
            qn = jnp.sum(qh * n0.astype(BF16).astype(F32), axis=1, keepdims=True)
            num = g * qc + qk * vh
            den = g * qn + qk
            hs.append(num / jnp.maximum(jnp.abs(den), jnp.exp(-mt)))
            kcol = _dot_nt(eye, jnp.broadcast_to(kh, (8, dh)).astype(BF16))[:, 0:1]
            c_ref[i, h] = g * c0 + (w * kcol) * vh
            n_ref[i, pl.ds(h, 1), :] = g * n0 + w * kh
            mrow = jnp.where(lane == h, mt, mrow)
        h_scr[pl.ds(i, 1), :] = jnp.concatenate(hs, axis=1)
        m_ref[pl.ds(i, 1), :] = mrow
        return 0

    lax.fori_loop(0, tb, body, 0)
    y_ref[...] = _rms(h_scr[...] * jax.nn.sigmoid(mo_ref[...]), gout_ref[...])


def _mlstm_decode(mq, mk, mv, mo, gcol, c0, n0, m0, wl, *, tb):
    n = mq.shape[0]
    nh = wl['m_heads']
    dh = mo.shape[1] // nh
    m0p = jnp.pad(m0, ((0, 0), (0, HEAD_PAD - nh)))
    tok = lambda w: pl.BlockSpec((tb, w), lambda i: (i, 0))
    kern = functools.partial(_mlstm_decode_kernel, n_heads=nh, tb=tb, dh=dh)
    y, c, nn, m = pl.pallas_call(
        kern,
        out_shape=[jax.ShapeDtypeStruct((n, nh * dh), F32), jax.ShapeDtypeStruct(c0.shape, F32),
                   jax.ShapeDtypeStruct(n0.shape, F32), jax.ShapeDtypeStruct((n, HEAD_PAD), F32)],
        grid=(n // tb,),
        in_specs=[tok(nh * HEAD_PAD), tok(nh * HEAD_PAD), tok(nh * HEAD_PAD), tok(nh * dh), tok(HEAD_PAD),
                  pl.BlockSpec((tb, nh, dh, dh), lambda i: (i, 0, 0, 0)),
                  pl.BlockSpec((tb, nh, dh), lambda i: (i, 0, 0)), tok(HEAD_PAD),
                  _const_spec(wl['g_mlstm'].shape)],
        out_specs=[tok(nh * dh), pl.BlockSpec((tb, nh, dh, dh), lambda i: (i, 0, 0, 0)),
                   pl.BlockSpec((tb, nh, dh), lambda i: (i, 0, 0)), tok(HEAD_PAD)],
        scratch_shapes=[pltpu.VMEM((tb, nh * HEAD_PAD), F32)] * 3 + [pltpu.VMEM((tb, nh * dh), F32)],
        compiler_params=_params(("parallel",)), name="mlstm_decode")(
        mq, mk, mv, mo, gcol, c0, n0, m0p, wl['g_mlstm'])
    return y, c, nn, m[:, :nh]


def _outffn_kernel(x_ref, ya_ref, yb_ref, yc_ref, gmla_ref, wout_ref, gpost_ref, gfpre_ref, wg_ref, wu_ref, wd_ref,
                   gfpost_ref, o_ref, *, ff_chunk):
    yb = _rms(yb_ref[...], gmla_ref[...])
    y = jnp.concatenate([ya_ref[...], yb, yc_ref[...]], axis=1).astype(BF16)
    x1 = x_ref[...] + _rms(_dot(y, wout_ref[...]), gpost_ref[...])
    h2 = _rms(x1, gfpre_ref[...]).astype(BF16)
    dff = wg_ref.shape[1]
    f = None
    for c0 in range(0, dff, ff_chunk):
        a = jax.nn.silu(_dot(h2, wg_ref[:, c0:c0 + ff_chunk])) * _dot(h2, wu_ref[:, c0:c0 + ff_chunk])
        part = _dot(a.astype(BF16), wd_ref[c0:c0 + ff_chunk, :])
        f = part if f is None else f + part
    o_ref[...] = x1 + _rms(f, gfpost_ref[...])


def _outffn(x, ya, yb, yc, wl, *, tm):
    n, d = x.shape
    weights = [wl['g_mla'], wl['wout'], wl['g_post'], wl['g_fpre'], wl['wg'], wl['wu'], wl['wd'], wl['g_fpost']]
    row = lambda w: pl.BlockSpec((tm, w), lambda i: (i, 0))
    dff = wl['wg'].shape[1]
    kern = functools.partial(_outffn_kernel, ff_chunk=dff // 2)
    return pl.pallas_call(
        kern, out_shape=jax.ShapeDtypeStruct((n, d), F32), grid=(n // tm,),
        in_specs=[row(d), row(ya.shape[1]), row(yb.shape[1]), row(yc.shape[1])]
        + [_const_spec(w.shape) for w in weights],
        out_specs=row(d), compiler_params=_params(("parallel",)), name="outffn")(x, ya, yb, yc, *weights)


def _rope_tables(positions):
    pos = np.asarray(positions, dtype=np.float64)[:, None]
    inv = ROPE_THETA ** (-np.arange(ROT, dtype=np.float64) / ROT)
    cos, sin = np.cos(pos * inv), np.sin(pos * inv)
    n = pos.shape[0]

    c = np.zeros((n, HEAD_PAD))
    lo = np.zeros((n, HEAD_PAD))
    hi = np.zeros((n, HEAD_PAD))
    c[:, 0:ROT] = cos
    c[:, ROT:2 * ROT] = cos
    lo[:, 0:ROT] = -sin
    hi[:, ROT:2 * ROT] = sin
    qc = np.zeros((HEAD_PAD, n))
    qs = np.zeros((HEAD_PAD, n))
    qc[:MLA_NOPE] = 1.0
    qc[MLA_NOPE:MLA_NOPE + ROT] = cos.T
    qc[MLA_NOPE + ROT:MLA_NOPE + 2 * ROT] = cos.T
    qs[MLA_NOPE:MLA_NOPE + ROT] = -sin.T
    qs[MLA_NOPE + ROT:MLA_NOPE + 2 * ROT] = sin.T
    return [jnp.asarray(a, dtype=F32) for a in (c, lo, hi, qc, qs)]


def _pad_heads(w, n_heads):
    rows = w.shape[0]
    w3 = w.reshape(rows, n_heads, -1)
    return jnp.pad(w3, ((0, 0), (0, 0), (0, HEAD_PAD - w3.shape[2]))).reshape(rows, n_heads * HEAD_PAD)


def _prep_layer(l, p):
    f = lambda name: p[name][l]
    row = lambda a: a.reshape(1, -1).astype(F32)
    w_in = f('w_in')
    d = w_in.shape[0]
    nh = p['mla_w_uk'].shape[2]
    mh = p['mlstm_b_i'].shape[1]
    o = 0
    cols = {}
    for name, width in (('u', 256), ('cq', 384), ('ckv', 256), ('kr', MLA_ROPE), ('mq', 256), ('mk', 256),
                        ('mv', 256), ('mo', 256), ('mi', mh), ('mf', mh)):
        cols[name] = w_in[:, o:o + width]
        o += width
    tail = jnp.concatenate([cols['kr'], cols['mi'], cols['mf'],
                            jnp.zeros((d, HEAD_PAD - MLA_ROPE - 2 * mh), F32)], axis=1)
    wz = jnp.concatenate([cols['u'], cols['cq'], cols['ckv'], _pad_heads(cols['mq'], mh), _pad_heads(cols['mk'], mh),
                          _pad_heads(cols['mv'], mh), cols['mo'], tail], axis=1).astype(BF16)
    bcol = jnp.zeros((1, HEAD_PAD), F32).at[0, TAIL_IG:TAIL_IG + mh].set(f('mlstm_b_i'))
    bcol = bcol.at[0, TAIL_LF:TAIL_LF + mh].set(f('mlstm_b_f'))
    brow = jnp.concatenate([f('mlstm_b_i'), f('mlstm_b_f')]).reshape(2 * mh, 1)
    one_lane = lambda n: jnp.zeros((n, HEAD_PAD), F32).at[:, MLA_NOPE].set(1.0).reshape(1, n * HEAD_PAD)

    w_uk = f('mla_w_uk')
    w_uv = f('mla_w_uv')
    r = w_uk.shape[0]
    wk_top = jnp.pad(w_uk, ((0, 0), (0, 0), (0, HEAD_PAD - MLA_NOPE))).reshape(r, nh * HEAD_PAD)
    place = jnp.zeros((HEAD_PAD, nh, HEAD_PAD), F32)
    e = jnp.arange(MLA_ROPE)
    place = place.at[e, :, MLA_NOPE + e].set(1.0)

    g_ = p['ssm_a_re'].shape[1]
    eye_g = jnp.eye(g_, dtype=F32)
    bre = jnp.einsum('gpc,gh->gchp', f('ssm_b_re'), eye_g).reshape(256, -1)
    bim = jnp.einsum('gpc,gh->gchp', f('ssm_b_im'), eye_g).reshape(256, -1)
    cre = jnp.einsum('gcp,gh->gphc', f('ssm_c_re'), eye_g).reshape(-1, 256)
    cim = jnp.einsum('gcp,gh->gphc', f('ssm_c_im'), eye_g).reshape(-1, 256)
    ns = bre.shape[1]
    return dict(
        n_heads=nh, m_heads=mh,
        g_pre=row(f('norm_mix_pre')), wz=wz,
        wgt=jnp.concatenate([cols['mi'], cols['mf']], axis=1).T.astype(BF16),
        wkt=cols['mk'].T.astype(BF16), bcol=bcol, brow=brow,
        qn=row(f('mla_q_norm')), wuqt=_pad_heads(f('mla_w_uq'), nh).T.astype(BF16), kvn=row(f('mla_kv_norm')),
        wk_top=wk_top.astype(BF16), wk_bot=place.reshape(HEAD_PAD, nh * HEAD_PAD).astype(BF16),
        wvt=jnp.pad(w_uv, ((0, 0), (0, 0), (0, HEAD_PAD - w_uv.shape[2]))).reshape(r, nh * HEAD_PAD).T.astype(BF16),
        vone=one_lane(nh).reshape(nh * HEAD_PAD, 1), vonem=one_lane(mh),
        wuk_h=w_uk.transpose(1, 0, 2).astype(BF16),
        wuv_h=w_uv.transpose(1, 0, 2).astype(BF16),
        ssm_are=row(f('ssm_a_re')), ssm_aim=row(f('ssm_a_im')),
        ssm_ldt=jnp.repeat(f('ssm_log_dt'), ns // g_).reshape(1, ns),
        ssm_bre=bre, ssm_bim=bim, ssm_ccat=jnp.concatenate([cre, -cim], axis=0).astype(BF16),
        ssm_d=row(f('ssm_d')), ssm_wglu=f('ssm_w_glu').astype(BF16), ssm_bglu=row(f('ssm_b_glu')),
        g_ssm=row(f('out_norm_ssm')), g_mla=row(f('out_norm_mla')), g_mlstm=row(f('out_norm_mlstm')),
        wout=f('w_out').astype(BF16), g_post=row(f('norm_mix_post')), g_fpre=row(f('norm_ffn_pre')),
        wg=f('ffn_w_gate').astype(BF16), wu=f('ffn_w_up').astype(BF16), wd=f('ffn_w_down').astype(BF16),
        g_fpost=row(f('norm_ffn_post')))


def kernel(x_prompt, x_sample, cache_ckv, cache_krope, page_table, state_ssm_re, state_ssm_im, state_mlstm_C, state_mlstm_n, state_mlstm_m, norm_mix_pre, norm_mix_post, norm_ffn_pre, norm_ffn_post, w_in, ssm_a_re, ssm_a_im, ssm_b_re, ssm_b_im, ssm_c_re, ssm_c_im, ssm_d, ssm_log_dt, ssm_w_glu, ssm_b_glu, mla_q_norm, mla_w_uq, mla_kv_norm, mla_w_uk, mla_w_uv, mlstm_b_i, mlstm_b_f, out_norm_ssm, out_norm_mla, out_norm_mlstm, w_out, ffn_w_gate, ffn_w_up, ffn_w_down):
    p = dict(norm_mix_pre=norm_mix_pre, norm_mix_post=norm_mix_post, norm_ffn_pre=norm_ffn_pre,
             norm_ffn_post=norm_ffn_post, w_in=w_in, ssm_a_re=ssm_a_re, ssm_a_im=ssm_a_im, ssm_b_re=ssm_b_re,
             ssm_b_im=ssm_b_im, ssm_c_re=ssm_c_re, ssm_c_im=ssm_c_im, ssm_d=ssm_d, ssm_log_dt=ssm_log_dt,
             ssm_w_glu=ssm_w_glu, ssm_b_glu=ssm_b_glu, mla_q_norm=mla_q_norm, mla_w_uq=mla_w_uq,
             mla_kv_norm=mla_kv_norm, mla_w_uk=mla_w_uk, mla_w_uv=mla_w_uv, mlstm_b_i=mlstm_b_i,
             mlstm_b_f=mlstm_b_f, out_norm_ssm=out_norm_ssm, out_norm_mla=out_norm_mla,
             out_norm_mlstm=out_norm_mlstm, w_out=w_out, ffn_w_gate=ffn_w_gate, ffn_w_up=ffn_w_up,
             ffn_w_down=ffn_w_down)
    depth = w_in.shape[0]
    bsz, seq, d = x_prompt.shape
    db, dec_t, _ = x_sample.shape
    assert dec_t == 1, "the sample group advances one token per sequence"
    n_pages, page = page_table.shape[1], cache_ckv.shape[2]
    past = n_pages * page
    g_, ns_g = ssm_a_re.shape[1], ssm_a_re.shape[2]
    ns = g_ * ns_g

    tm_p = min(TM_PROJ, seq)
    tm_f = min(TM_FFN, seq)
    tc = min(T_SSM, seq)
    tq = min(T_ATT, seq)
    tl = min(T_MLSTM, seq)
    tb = min(TOK_MLSTM_DEC, db)
    pps = min(PAGES_PER_STEP, n_pages)
    assert seq % tm_p == 0 and seq % tm_f == 0 and seq % tc == 0 and seq % tq == 0 and seq % tl == 0
    assert db % tb == 0 and n_pages % pps == 0

    tabs_p = _rope_tables(np.arange(seq))
    tabs_s = _rope_tables(np.full((db,), past))
    pt_flat = page_table.reshape(-1).astype(jnp.int32)
    cache_krt = jnp.swapaxes(cache_krope, 2, 3)

    xp = x_prompt.reshape(bsz * seq, d)
    xs = x_sample.reshape(db, d)
    outs_p, outs_s = [], []
    for l in range(depth):
        wl = _prep_layer(l, p)
        nh, mh = wl['n_heads'], wl['m_heads']
        dv = wl['wuv_h'].shape[2]

        (u, qT, ckvn, kr, kpad, vT, mq, mk, mv, mo, gcol, grow, kT) = _inproj(
            xp, wl, tabs_p, tm=tm_p, tk=tq, tiles_per_seq=seq // tm_p)
        ya, s_re, s_im = _ssm_prompt(u, wl, batch=bsz, seq=seq, tc=tc)
        yb = _attn_prompt(qT, kpad, vT, batch=bsz, seq=seq, n_heads=nh, tq=tq, dv=dv)
        yc, c_p, n_p, m_p = _mlstm_prompt(mq, mk, kT, mv, mo, gcol, grow, wl, batch=bsz, seq=seq, tl=tl)
        xp = _outffn(xp, ya, yb, yc, wl, tm=tm_f)
        outs_p.append((ckvn.reshape(bsz, seq, -1), kr.reshape(bsz, seq, -1),
                       s_re.reshape(bsz, g_, ns_g), s_im.reshape(bsz, g_, ns_g), c_p, n_p, m_p))

        (u, qT, ckvn, kr, _, _, mq, mk, mv, mo, gcol, _, _) = _inproj(
            xs, wl, tabs_s, tm=db, tk=db, tiles_per_seq=1)
        ya, h_re, h_im = _ssm_decode(u, state_ssm_re[l].reshape(db, ns), state_ssm_im[l].reshape(db, ns), wl)
        qlat, qrope = _qlat(qT, wl)
        olat = _dattn(qlat.transpose(2, 0, 1), qrope.transpose(2, 0, 1), ckvn.reshape(db, 1, -1),
                      kr.reshape(db, 1, -1), cache_ckv, cache_krt, pt_flat, layer=l, n_pages=n_pages, pps=pps)
        yb = _uv(olat.transpose(1, 0, 2), wl)
        yc, c_s, n_s, m_s = _mlstm_decode(mq, mk, mv, mo, gcol, state_mlstm_C[l], state_mlstm_n[l],
                                          state_mlstm_m[l], wl, tb=tb)
        xs = _outffn(xs, ya, yb, yc, wl, tm=db)
        outs_s.append((ckvn.reshape(db, 1, -1), kr.reshape(db, 1, -1),
                       h_re.reshape(db, g_, ns_g), h_im.reshape(db, g_, ns_g), c_s, n_s, m_s))

    stk = lambda lst, i: jnp.stack([s[i] for s in lst], axis=0)
    return (xp.reshape(bsz, seq, d), xs.reshape(db, 1, d),
            stk(outs_p, 0), stk(outs_p, 1), stk(outs_s, 0), stk(outs_s, 1),
            stk(outs_p, 2), stk(outs_p, 3), stk(outs_s, 2), stk(outs_s, 3),
            stk(outs_p, 4), stk(outs_p, 5), stk(outs_p, 6),
            stk(outs_s, 4), stk(outs_s, 5), stk(outs_s, 6))
```

```python
import functools
import math

import numpy as np
import jax
import jax.numpy as jnp
from jax import lax
from jax.experimental import pallas as pl
from jax.experimental.pallas import tpu as pltpu

F32 = jnp.float32
BF16 = jnp.bfloat16

RMS_EPS = 1e-6
ROPE_THETA = 10000.0
MLA_NOPE = 64
MLA_ROPE = 32
LANES = 128
HEAD_PAD = LANES
ROT = MLA_ROPE // 2

TM_PROJ = 512
TM_FFN = 512
T_SSM = 512
T_ATT = 256
T_MLSTM = 256
TOK_MLSTM_DEC = 16
PAGES_PER_STEP = 32
N_PAGE_BUFS = 3
VMEM_LIMIT = 56 * 1024 * 1024

NT_DIMS = (((1,), (1,)), ((), ()))


def _dot(a, b):
    return jnp.dot(a, b, preferred_element_type=F32)


def _dot_nt(a, b):
    return lax.dot_general(a, b, NT_DIMS, preferred_element_type=F32)


def _rms(x, g):
    return x * lax.rsqrt(jnp.mean(x * x, axis=-1, keepdims=True) + RMS_EPS) * g


def _log_sigmoid(x):
    return jnp.minimum(x, 0.0) - jnp.log1p(jnp.exp(-jnp.abs(x)))


def _rope_lanes(c, cos_t, sin_lo, sin_hi):
    return c * cos_t + pltpu.roll(c, HEAD_PAD - ROT, 1) * sin_lo + pltpu.roll(c, ROT, 1) * sin_hi


def _const_spec(shape):
    nd = len(shape)
    return pl.BlockSpec(shape, lambda *_: (0,) * nd, pipeline_mode=pl.Buffered(1))


def _params(sem):
    return pltpu.CompilerParams(dimension_semantics=sem, vmem_limit_bytes=VMEM_LIMIT)


Z_U, Z_CQ, Z_CKV, Z_MQ, Z_MK, Z_MV, Z_MO, Z_TAIL, Z_END = 0, 256, 640, 896, 1408, 1920, 2432, 2688, 2816
TAIL_IG = MLA_ROPE
TAIL_LF = MLA_ROPE + 4


def _inproj_kernel(x_ref, gpre_ref, wz_ref, wgt_ref, wkt_ref, bcol_ref, brow_ref,
                   tcos_ref, tslo_ref, tshi_ref, qcos_ref, qsin_ref,
                   qn_ref, wuqt_ref, kvn_ref, wkt_top_ref, wkt_bot_ref, wvt_ref, vone_ref, vonem_ref,
                   u_ref, qT_ref, ckv_ref, kr_ref, kpad_ref, vT_ref,
                   mq_ref, mk_ref, mv_ref, mo_ref, gcol_ref, grow_ref, kT_ref, *, n_heads, scale, tk):
    hb = _rms(x_ref[...], gpre_ref[...]).astype(BF16)
    z = _dot(hb, wz_ref[...])

    u_ref[...] = z[:, Z_U:Z_CQ]
    mq_ref[...] = z[:, Z_MQ:Z_MK].astype(BF16)
    mk_ref[...] = (z[:, Z_MK:Z_MV] * 0.125).astype(BF16)
    mv_ref[...] = (z[:, Z_MV:Z_MO] + vonem_ref[...]).astype(BF16)
    mo_ref[...] = z[:, Z_MO:Z_TAIL]

    tail = z[:, Z_TAIL:Z_END]
    rot = _rope_lanes(tail, tcos_ref[...], tslo_ref[...], tshi_ref[...])
    kr_ref[...] = rot[:, :MLA_ROPE]

    gt = tail + bcol_ref[...]
    lane = lax.broadcasted_iota(jnp.int32, gt.shape, 1)
    gcol_ref[...] = jnp.where((lane >= TAIL_LF) & (lane < TAIL_LF + 4), _log_sigmoid(gt), gt)

    gr = _dot_nt(wgt_ref[...], hb) + brow_ref[...]
    row = lax.broadcasted_iota(jnp.int32, gr.shape, 0)
    grow_ref[...] = jnp.where(row >= 4, _log_sigmoid(gr), gr)

    kT_ref[...] = (_dot_nt(wkt_ref[...], hb) * 0.125).astype(BF16)

    cqn = _rms(z[:, Z_CQ:Z_CKV], qn_ref[...]).astype(BF16)
    qa = _dot_nt(wuqt_ref[...], cqn) * scale
    qcos, qsin = qcos_ref[...], qsin_ref[...]
    r0, r1, r2 = MLA_NOPE, MLA_NOPE + ROT, MLA_NOPE + 2 * ROT
    for h in range(n_heads):
        g = qa[h * HEAD_PAD:(h + 1) * HEAD_PAD, :]
        sh = jnp.concatenate([g[0:r0], g[r1:r2], g[r0:r1], g[r2:HEAD_PAD]], axis=0)
        qT_ref[h] = (g * qcos + sh * qsin).astype(BF16)

    ckvn = _rms(z[:, Z_CKV:Z_MQ], kvn_ref[...])
    ckv_ref[...] = ckvn
    cb = ckvn.astype(BF16)
    kp = (_dot(cb, wkt_top_ref[...]) + _dot(rot.astype(BF16), wkt_bot_ref[...])).astype(BF16)
    vt = (_dot_nt(wvt_ref[...], cb) + vone_ref[...]).astype(BF16)
    for h in range(n_heads):
        kpad_ref[h] = kp[:, h * HEAD_PAD:(h + 1) * HEAD_PAD]
        for c in range(vT_ref.shape[1]):
            vT_ref[h, c] = vt[h * HEAD_PAD:(h + 1) * HEAD_PAD, c * tk:(c + 1) * tk]


def _inproj(x, wl, tables, *, tm, tk, tiles_per_seq):
    n, d = x.shape
    nh = wl['n_heads']
    grid = (n // tm,)
    row = lambda w: pl.BlockSpec((tm, w), lambda i: (i, 0))
    tab = pl.BlockSpec((tm, HEAD_PAD), lambda i: (i % tiles_per_seq, 0))
    tab_t = pl.BlockSpec((HEAD_PAD, tm), lambda i: (0, i % tiles_per_seq))
    weights = [wl['g_pre'], wl['wz'], wl['wgt'], wl['wkt'], wl['bcol'], wl['brow']]
    mla_w = [wl['qn'], wl['wuqt'], wl['kvn'], wl['wk_top'], wl['wk_bot'], wl['wvt'], wl['vone'], wl['vonem']]
    in_specs = ([row(d)] + [_const_spec(w.shape) for w in weights] + [tab] * 3 + [tab_t] * 2
                + [_const_spec(w.shape) for w in mla_w])
    out_shape = [
        jax.ShapeDtypeStruct((n, 256), F32),
        jax.ShapeDtypeStruct((nh, HEAD_PAD, n), BF16),
        jax.ShapeDtypeStruct((n, 256), F32),
        jax.ShapeDtypeStruct((n, MLA_ROPE), F32),
        jax.ShapeDtypeStruct((nh, n, HEAD_PAD), BF16),
        jax.ShapeDtypeStruct((nh, n // tk, HEAD_PAD, tk), BF16),
        jax.ShapeDtypeStruct((n, 512), BF16),
        jax.ShapeDtypeStruct((n, 512), BF16),
        jax.ShapeDtypeStruct((n, 512), BF16),
        jax.ShapeDtypeStruct((n, 256), F32),
        jax.ShapeDtypeStruct((n, HEAD_PAD), F32),
        jax.ShapeDtypeStruct((8, n), F32),
        jax.ShapeDtypeStruct((256, n), BF16),
    ]
    out_specs = [row(256), pl.BlockSpec((nh, HEAD_PAD, tm), lambda i: (0, 0, i)), row(256), row(MLA_ROPE),
                 pl.BlockSpec((nh, tm, HEAD_PAD), lambda i: (0, i, 0)),
                 pl.BlockSpec((nh, tm // tk, HEAD_PAD, tk), lambda i: (0, i, 0, 0)),
                 row(512), row(512), row(512), row(256), row(HEAD_PAD),
                 pl.BlockSpec((8, tm), lambda i: (0, i)), pl.BlockSpec((256, tm), lambda i: (0, i))]
    scale = (MLA_NOPE + MLA_ROPE) ** -0.5 * math.log2(math.e)
    kern = functools.partial(_inproj_kernel, n_heads=nh, scale=scale, tk=tk)
    return pl.pallas_call(kern, out_shape=out_shape, grid=grid, in_specs=in_specs, out_specs=out_specs,
                          compiler_params=_params(("parallel",)), name="inproj")(
        x, *weights, *tables, *mla_w)


def _ssm_discretise(are, aim, ldt):
    dt = jnp.exp(ldt)
    mag = jnp.exp(are * dt)
    lbr = mag * jnp.cos(aim * dt)
    lbi = mag * jnp.sin(aim * dt)
    inv = 1.0 / (are * are + aim * aim)
    fr = ((lbr - 1.0) * are + lbi * aim) * inv
    fi = (lbi * are - (lbr - 1.0) * aim) * inv
    return lbr, lbi, fr, fi


def _ssm_input_matrix(fr, fi, bre, bim):
    return jnp.concatenate([fr * bre - fi * bim, fr * bim + fi * bre], axis=1).astype(BF16)


def _ssm_tail(xcat, u, ccat_ref, d_ref, wglu_ref, bglu_ref, gout_ref):
    y = _dot(xcat.astype(BF16), ccat_ref[...]) + d_ref[...] * u
    y = jax.nn.gelu(y)
    y = y * jax.nn.sigmoid(_dot(y.astype(BF16), wglu_ref[...]) + bglu_ref[...])
    return _rms(y, gout_ref[...])


def _cmul(ar, ai, br, bi):
    return ar * br - ai * bi, ar * bi + ai * br


def _ssm_prompt_kernel(u_ref, are_ref, aim_ref, ldt_ref, bre_ref, bim_ref, ccat_ref, d_ref, wglu_ref, bglu_ref,
                       gout_ref, y_ref, st_ref, x_scr, car_scr, bb_scr, coef_scr, *, tc, ns):
    c = pl.program_id(1)

    @pl.when(c == 0)
    def _():
        car_scr[...] = jnp.zeros_like(car_scr)
        lbr, lbi, fr, fi = _ssm_discretise(are_ref[...], aim_ref[...], ldt_ref[...])
        bb_scr[...] = _ssm_input_matrix(fr, fi, bre_ref[...], bim_ref[...])
        p = [(lbr, lbi)]
        for _ in range(7):
            p.append(_cmul(p[-1][0], p[-1][1], lbr, lbi))
        rowi = lax.broadcasted_iota(jnp.int32, (8, ns), 0)
        pwr = jnp.zeros((8, ns), F32)
        pwi = jnp.zeros((8, ns), F32)
        for j in range(8):
            pwr = jnp.where(rowi == j, p[j][0], pwr)
            pwi = jnp.where(rowi == j, p[j][1], pwi)
        coef_scr[0] = pwr
        coef_scr[1] = pwi
        for i, dd in enumerate((1, 2, 4)):
            coef_scr[2 + 2 * i] = jnp.where(rowi >= dd, p[dd - 1][0], 0.0)
            coef_scr[3 + 2 * i] = jnp.where(rowi >= dd, p[dd - 1][1], 0.0)

    u = u_ref[...]
    x_scr[...] = _dot(u.astype(BF16), bb_scr[...])
    pwr, pwi = coef_scr[0], coef_scr[1]
    coefs = [(dd, coef_scr[2 + 2 * i], coef_scr[3 + 2 * i]) for i, dd in enumerate((1, 2, 4))]

    def body(g, carry):
        hr, hi = carry
        r0 = pl.multiple_of(g * 8, 8)
        xr = x_scr[pl.ds(r0, 8), 0:ns]
        xi = x_scr[pl.ds(r0, 8), ns:2 * ns]
        for dd, ar, ai in coefs:
            sr = pltpu.roll(xr, dd, 0)
            si = pltpu.roll(xi, dd, 0)
            xr, xi = xr + ar * sr - ai * si, xi + ar * si + ai * sr
        xr = xr + pwr * hr - pwi * hi
        xi = xi + pwr * hi + pwi * hr
        x_scr[pl.ds(r0, 8), 0:ns] = xr
        x_scr[pl.ds(r0, 8), ns:2 * ns] = xi
        return xr[7:8, :], xi[7:8, :]

    hr, hi = lax.fori_loop(0, tc // 8, body, (car_scr[0:1, 0:ns], car_scr[0:1, ns:2 * ns]))
    car_scr[0:1, 0:ns] = hr
    car_scr[0:1, ns:2 * ns] = hi
    st_ref[0:1, 0:ns] = hr
    st_ref[0:1, ns:2 * ns] = hi
    y_ref[...] = _ssm_tail(x_scr[...], u, ccat_ref, d_ref, wglu_ref, bglu_ref, gout_ref)


def _ssm_prompt(u, wl, *, batch, seq, tc):
    n, w = u.shape
    ns = wl['ssm_are'].shape[1]
    nt = seq // tc
    weights = [wl['ssm_are'], wl['ssm_aim'], wl['ssm_ldt'], wl['ssm_bre'], wl['ssm_bim'], wl['ssm_ccat'],
               wl['ssm_d'], wl['ssm_wglu'], wl['ssm_bglu'], wl['g_ssm']]
    kern = functools.partial(_ssm_prompt_kernel, tc=tc, ns=ns)
    y, st = pl.pallas_call(
        kern,
        out_shape=[jax.ShapeDtypeStruct((n, w), F32), jax.ShapeDtypeStruct((batch, 1, 2 * ns), F32)],
        grid=(batch, nt),
        in_specs=[pl.BlockSpec((tc, w), lambda b, c: (b * nt + c, 0))] + [_const_spec(a.shape) for a in weights],
        out_specs=[pl.BlockSpec((tc, w), lambda b, c: (b * nt + c, 0)),
                   pl.BlockSpec((None, 1, 2 * ns), lambda b, c: (b, 0, 0))],
        scratch_shapes=[pltpu.VMEM((tc, 2 * ns), F32), pltpu.VMEM((8, 2 * ns), F32),
                        pltpu.VMEM((w, 2 * ns), BF16), pltpu.VMEM((8, 8, ns), F32)],
        compiler_params=_params(("parallel", "arbitrary")), name="ssm_prompt")(u, *weights)
    return y, st[:, 0, :ns], st[:, 0, ns:]


def _ssm_decode_kernel(u_ref, h0r_ref, h0i_ref, are_ref, aim_ref, ldt_ref, bre_ref, bim_ref, ccat_ref, d_ref,
                       wglu_ref, bglu_ref, gout_ref, y_ref, hr_ref, hi_ref, *, ns):
    lbr, lbi, fr, fi = _ssm_discretise(are_ref[...], aim_ref[...], ldt_ref[...])
    u = u_ref[...]
    bu = _dot(u.astype(BF16), _ssm_input_matrix(fr, fi, bre_ref[...], bim_ref[...]))
    h0r, h0i = h0r_ref[...], h0i_ref[...]
    hr = bu[:, 0:ns] + lbr * h0r - lbi * h0i
    hi = bu[:, ns:2 * ns] + lbr * h0i + lbi * h0r
    hr_ref[...] = hr
    hi_ref[...] = hi
    y_ref[...] = _ssm_tail(jnp.concatenate([hr, hi], axis=1), u, ccat_ref, d_ref, wglu_ref, bglu_ref, gout_ref)


def _ssm_decode(u, h0r, h0i, wl):
    n, w = u.shape
    ns = wl['ssm_are'].shape[1]
    weights = [wl['ssm_are'], wl['ssm_aim'], wl['ssm_ldt'], wl['ssm_bre'], wl['ssm_bim'], wl['ssm_ccat'],
               wl['ssm_d'], wl['ssm_wglu'], wl['ssm_bglu'], wl['g_ssm']]
    kern = functools.partial(_ssm_decode_kernel, ns=ns)
    return pl.pallas_call(
        kern,
        out_shape=[jax.ShapeDtypeStruct((n, w), F32), jax.ShapeDtypeStruct((n, ns), F32),
                   jax.ShapeDtypeStruct((n, ns), F32)],
        compiler_params=pltpu.CompilerParams(vmem_limit_bytes=VMEM_LIMIT), name="ssm_decode")(u, h0r, h0i, *weights)


def _attn_kernel(qT_ref, k_ref, vT_ref, o_ref, m_scr, acc_scr, sa_scr, sb_scr, *, n_heads, tq, dv):
    qi = pl.program_id(1)
    keyi = lax.broadcasted_iota(jnp.int32, (tq, tq), 0)
    qryi = lax.broadcasted_iota(jnp.int32, (tq, tq), 1)
    causal = keyi <= qryi
    heads = range(n_heads)

    def scores(j, dst):
        k0 = pl.multiple_of(j * tq, tq)
        for h in heads:
            dst[h] = _dot(k_ref[h, pl.ds(k0, tq), :], qT_ref[h])

    def softmax_values(j, src, mask):
        ps, alphas = [], []
        for h in heads:
            s = src[h]
            if mask:
                s = jnp.where(causal, s, -jnp.inf)
            m_old = m_scr[h]
            m_new = jnp.maximum(m_old, jnp.max(s, axis=0, keepdims=True))
            ps.append(jnp.exp2(s - m_new).astype(BF16))
            alphas.append(jnp.exp2(m_old - m_new))
            m_scr[h] = m_new
        for h in heads:
            acc_scr[h] = alphas[h] * acc_scr[h] + _dot(vT_ref[h, j], ps[h])

    m_scr[...] = jnp.full(m_scr.shape, -jnp.inf, F32)
    acc_scr[...] = jnp.zeros_like(acc_scr)
    scores(0, sa_scr)

    def body(t, _):
        j = 2 * t
        scores(j + 1, sb_scr)
        softmax_values(j, sa_scr, False)
        scores(j + 2, sa_scr)
        softmax_values(j + 1, sb_scr, False)
        return 0

    lax.fori_loop(0, qi // 2, body, 0)

    @pl.when(qi % 2 == 0)
    def _():
        softmax_values(qi, sa_scr, True)

    @pl.when(qi % 2 == 1)
    def _():
        scores(qi, sb_scr)
        softmax_values(qi - 1, sa_scr, False)
        softmax_values(qi, sb_scr, True)

    outs = []
    for h in heads:
        acc = acc_scr[h]
        outs.append(acc[:dv, :] / acc[dv:dv + 1, :])
    o_ref[...] = jnp.concatenate(outs, axis=0).T


def _attn_prompt(qT, k, vT, *, batch, seq, n_heads, tq, dv):
    n = k.shape[1]
    nq = seq // tq
    kern = functools.partial(_attn_kernel, n_heads=n_heads, tq=tq, dv=dv)
    return pl.pallas_call(
        kern,
        out_shape=jax.ShapeDtypeStruct((n, n_heads * dv), F32),
        grid=(batch, nq),
        in_specs=[pl.BlockSpec((n_heads, HEAD_PAD, tq), lambda b, i: (0, 0, b * nq + i)),
                  pl.BlockSpec((n_heads, seq, HEAD_PAD), lambda b, i: (0, b, 0)),
                  pl.BlockSpec((n_heads, nq, HEAD_PAD, tq), lambda b, i: (0, b, 0, 0))],
        out_specs=pl.BlockSpec((tq, n_heads * dv), lambda b, i: (b * nq + i, 0)),
        scratch_shapes=[pltpu.VMEM((n_heads, 1, tq), F32), pltpu.VMEM((n_heads, HEAD_PAD, tq), F32),
                        pltpu.VMEM((n_heads, tq, tq), F32), pltpu.VMEM((n_heads, tq, tq), F32)],
        compiler_params=_params(("parallel", "arbitrary")), name="attn_prompt")(qT, k, vT)


def _qlat_kernel(qT_ref, wuk_ref, qlat_ref, qrope_ref, *, n_heads):
    for h in range(n_heads):
        qlat_ref[h] = _dot(wuk_ref[h], qT_ref[h, 0:MLA_NOPE, :]).astype(BF16)
        qrope_ref[h] = qT_ref[h, MLA_NOPE:MLA_NOPE + MLA_ROPE, :]


def _qlat(qT, wl):
    nh, _, n = qT.shape
    r = wl['wuk_h'].shape[1]
    kern = functools.partial(_qlat_kernel, n_heads=nh)
    return pl.pallas_call(
        kern,
        out_shape=[jax.ShapeDtypeStruct((nh, r, n), BF16), jax.ShapeDtypeStruct((nh, MLA_ROPE, n), BF16)],
        compiler_params=pltpu.CompilerParams(vmem_limit_bytes=VMEM_LIMIT), name="qlat")(qT, wl['wuk_h'])


def _dattn_kernel(pt_ref, qlat_ref, qrope_ref, cnew_ref, knew_ref, ckv_hbm, krt_hbm, o_ref,
                  ckv_buf, krt_buf, sem, m_scr, l_scr, acc_scr, *, layer, n_pages, pps, nc, total):
    s_id, c = pl.program_id(0), pl.program_id(1)
    step = s_id * nc + c
    slot = step % N_PAGE_BUFS

    def page_copies(slot_, s_, c_):
        out = []
        for j in range(pps):
            page = pt_ref[s_ * n_pages + c_ * pps + j]
            out.append(pltpu.make_async_copy(ckv_hbm.at[layer, page], ckv_buf.at[slot_, j], sem.at[slot_]))
            out.append(pltpu.make_async_copy(krt_hbm.at[layer, page], krt_buf.at[slot_, j], sem.at[slot_]))
        return out

    def step_copies(lin):
        return page_copies(lin % N_PAGE_BUFS, lin // nc, lin % nc)

    @pl.when(step == 0)
    def _():
        for ahead in range(min(N_PAGE_BUFS - 1, total)):
            for cp in step_copies(ahead):
                cp.start()

    @pl.when(step + (N_PAGE_BUFS - 1) < total)
    def _():
        for cp in step_copies(step + (N_PAGE_BUFS - 1)):
            cp.start()

    ql = qlat_ref[...]
    qr = qrope_ref[...]

    @pl.when(c == 0)
    def _():
        cn = cnew_ref[...].astype(BF16).astype(F32)
        kn = knew_ref[...].astype(BF16).astype(F32)
        s_self = (jnp.sum(ql.astype(F32) * cn, axis=1, keepdims=True)
                  + jnp.sum(qr.astype(F32) * kn, axis=1, keepdims=True))
        m_scr[...] = s_self
        l_scr[...] = jnp.ones_like(l_scr)
        acc_scr[...] = jnp.broadcast_to(cn, acc_scr.shape)

    for cp in page_copies(slot, s_id, c):
        cp.wait()

    page = ckv_buf.shape[2]
    pages = [ckv_buf[slot, j].astype(BF16) for j in range(pps)]
    s = jnp.concatenate(
        [_dot_nt(ql, pages[j]) + _dot(qr, krt_buf[slot, j].astype(BF16)) for j in range(pps)], axis=1)
    m_old = m_scr[...]
    m_new = jnp.maximum(m_old, jnp.max(s, axis=1, keepdims=True))
    alpha = jnp.exp2(m_old - m_new)
    p = jnp.exp2(s - m_new)
    l_scr[...] = alpha * l_scr[...] + jnp.sum(p, axis=1, keepdims=True)
    pb = p.astype(BF16)
    pv = _dot(pb[:, 0:page], pages[0])
    for j in range(1, pps):
        pv = pv + _dot(pb[:, j * page:(j + 1) * page], pages[j])
    acc_scr[...] = alpha * acc_scr[...] + pv
    m_scr[...] = m_new

    @pl.when(c == nc - 1)
    def _():
        o_ref[...] = acc_scr[...] / l_scr[...]


def _dattn(qlat, qrope, cnew, knew, cache_ckv, cache_krt, pt_flat, *, layer, n_pages, pps):
    nb, nh, r = qlat.shape
    page = cache_ckv.shape[2]
    nc = n_pages // pps
    per_seq = lambda shape: pl.BlockSpec((None,) + shape, lambda s, c, pt: (s, 0, 0))
    hbm = pl.BlockSpec(memory_space=pl.ANY)
    grid_spec = pltpu.PrefetchScalarGridSpec(
        num_scalar_prefetch=1, grid=(nb, nc),
        in_specs=[per_seq((nh, r)), per_seq((nh, MLA_ROPE)), per_seq((1, r)), per_seq((1, MLA_ROPE)), hbm, hbm],
        out_specs=per_seq((nh, r)),
        scratch_shapes=[pltpu.VMEM((N_PAGE_BUFS, pps, page, r), F32),
                        pltpu.VMEM((N_PAGE_BUFS, pps, MLA_ROPE, page), F32),
                        pltpu.SemaphoreType.DMA((N_PAGE_BUFS,)),
                        pltpu.VMEM((nh, 1), F32), pltpu.VMEM((nh, 1), F32), pltpu.VMEM((nh, r), F32)])
    kern = functools.partial(_dattn_kernel, layer=layer, n_pages=n_pages, pps=pps, nc=nc, total=nb * nc)
    return pl.pallas_call(
        kern, out_shape=jax.ShapeDtypeStruct((nb, nh, r), F32), grid_spec=grid_spec,
        compiler_params=_params(("arbitrary", "arbitrary")), name="attn_decode")(
        pt_flat, qlat, qrope, cnew, knew, cache_ckv, cache_krt)


def _uv_kernel(o_ref, wuv_ref, y_ref, *, n_heads):
    y_ref[...] = jnp.concatenate([_dot(o_ref[h].astype(BF16), wuv_ref[h]) for h in range(n_heads)], axis=1)


def _uv(olat_hm, wl):
    nh, n, _ = olat_hm.shape
    dv = wl['wuv_h'].shape[2]
    kern = functools.partial(_uv_kernel, n_heads=nh)
    return pl.pallas_call(kern, out_shape=jax.ShapeDtypeStruct((n, nh * dv), F32),
                          compiler_params=pltpu.CompilerParams(vmem_limit_bytes=VMEM_LIMIT), name="uv")(
        olat_hm, wl['wuv_h'])


def _split3(x):
    hi = x.astype(BF16)
    r1 = x - hi.astype(F32)
    mid = r1.astype(BF16)
    lo = (r1 - mid.astype(F32)).astype(BF16)
    return hi, mid, lo


def _mlstm_prompt_kernel(q_ref, k_ref, kT_ref, v_ref, mo_ref, gcol_ref, grow_ref, gout_ref,
                         y_ref, cn_ref, m_ref, cn_scr, m_scr, *, n_heads, tl, dh):
    c = pl.program_id(1)

    @pl.when(c == 0)
    def _():
        cn_scr[...] = jnp.zeros_like(cn_scr)
        m_scr[...] = jnp.zeros_like(m_scr)

    rowi = lax.broadcasted_iota(jnp.int32, (tl, tl), 0)
    coli = lax.broadcasted_iota(jnp.int32, (tl, tl), 1)
    causal = coli <= rowi
    tril = causal.astype(BF16)
    triu = (rowi <= coli).astype(BF16)
    gcol = gcol_ref[...]
    grow = grow_ref[...]
    bcol = sum(_dot(tril, part) for part in _split3(gcol))
    brow = sum(_dot(part, triu) for part in _split3(grow))
    heads = range(n_heads)
    hs = [slice(h * HEAD_PAD, (h + 1) * HEAD_PAD) for h in heads]
    cns = [cn_scr[h] for h in heads]
    s_qk = [_dot_nt(q_ref[:, hs[h]], k_ref[:, hs[h]]) for h in heads]
    q_c = [_dot(q_ref[:, h * HEAD_PAD:h * HEAD_PAD + dh], cns[h].astype(BF16)) for h in heads]
    qks, kws, gs, mts = [], [], [], []
    for h in heads:
        bc = bcol[:, TAIL_LF + h:TAIL_LF + h + 1]
        br = brow[4 + h:5 + h, :]
        ir = grow[h:h + 1, :]
        m_prev = m_scr[h:h + 1, 0:1]
        lw = jnp.where(causal, bc - (br - ir), -jnp.inf)
        m_inter = bc + m_prev
        mt = jnp.maximum(m_inter, jnp.max(lw, axis=1, keepdims=True))
        w = jnp.exp(lw - mt)
        gs.append(jnp.exp(m_inter - mt))
        mts.append(mt)
        qks.append((s_qk[h] * w).astype(BF16))
        kws.append((kT_ref[h * dh:(h + 1) * dh, :].astype(F32) * w[tl - 1:tl, :]).astype(BF16))
    qkv = [_dot(qks[h], v_ref[:, hs[h]]) for h in heads]
    kwv = [_dot(kws[h], v_ref[:, hs[h]]) for h in heads]
    outs = []
    for h in heads:
        tot = gs[h] * q_c[h] + qkv[h]
        den = tot[:, dh:dh + 1]
        outs.append(tot[:, :dh] / jnp.maximum(jnp.abs(den), jnp.exp(-mts[h])))
        cn_scr[h] = gs[h][tl - 1:tl, :] * cns[h] + kwv[h]
        m_scr[h:h + 1, :] = jnp.broadcast_to(mts[h][tl - 1:tl, :], (1, m_scr.shape[1]))
    hcat = jnp.concatenate(outs, axis=1)
    y_ref[...] = _rms(hcat * jax.nn.sigmoid(mo_ref[...]), gout_ref[...])
    cn_ref[...] = cn_scr[...]
    m_ref[...] = m_scr[...]


def _mlstm_prompt(mq, mk, kT, mv, mo, gcol, grow, wl, *, batch, seq, tl):
    n = mq.shape[0]
    nh = wl['m_heads']
    dh = mo.shape[1] // nh
    nt = seq // tl
    tok = lambda w: pl.BlockSpec((tl, w), lambda b, c: (b * nt + c, 0))
    kern = functools.partial(_mlstm_prompt_kernel, n_heads=nh, tl=tl, dh=dh)
    y, cn, m = pl.pallas_call(
        kern,
        out_shape=[jax.ShapeDtypeStruct((n, nh * dh), F32),
                   jax.ShapeDtypeStruct((batch, nh, dh, HEAD_PAD), F32),
                   jax.ShapeDtypeStruct((batch, 8, HEAD_PAD), F32)],
        grid=(batch, nt),
        in_specs=[tok(nh * HEAD_PAD), tok(nh * HEAD_PAD),
                  pl.BlockSpec((nh * dh, tl), lambda b, c: (0, b * nt + c)),
                  tok(nh * HEAD_PAD), tok(nh * dh), tok(HEAD_PAD),
                  pl.BlockSpec((8, tl), lambda b, c: (0, b * nt + c)),
                  _const_spec(wl['g_mlstm'].shape)],
        out_specs=[tok(nh * dh),
                   pl.BlockSpec((None, nh, dh, HEAD_PAD), lambda b, c: (b, 0, 0, 0)),
                   pl.BlockSpec((None, 8, HEAD_PAD), lambda b, c: (b, 0, 0))],
        scratch_shapes=[pltpu.VMEM((nh, dh, HEAD_PAD), F32), pltpu.VMEM((8, HEAD_PAD), F32)],
        compiler_params=_params(("parallel", "arbitrary")), name="mlstm_prompt")(
        mq, mk, kT, mv, mo, gcol, grow, wl['g_mlstm'])
    return y, cn[..., :dh], cn[..., dh], m[:, :nh, 0]


def _mlstm_decode_kernel(q_ref, k_ref, v_ref, mo_ref, gcol_ref, c0_ref, n0_ref, m0_ref, gout_ref,
                         y_ref, c_ref, n_ref, m_ref, q_scr, k_scr, v_scr, h_scr, *, n_heads, tb, dh):
    q_scr[...] = q_ref[...].astype(F32)
    k_scr[...] = k_ref[...].astype(F32)
    v_scr[...] = v_ref[...].astype(F32)
    eye = (lax.broadcasted_iota(jnp.int32, (dh, dh), 0) == lax.broadcasted_iota(jnp.int32, (dh, dh), 1)).astype(BF16)
    lane = lax.broadcasted_iota(jnp.int32, (1, HEAD_PAD), 1)

    def body(i, _):
        gates = gcol_ref[pl.ds(i, 1), :]
        m0row = m0_ref[pl.ds(i, 1), :]
        qrow = q_scr[pl.ds(i, 1), :]
        krow = k_scr[pl.ds(i, 1), :]
        vrow = v_scr[pl.ds(i, 1), :]
        mrow = jnp.zeros((1, HEAD_PAD), F32)
        hs = []
        for h in range(n_heads):
            qh = qrow[:, h * HEAD_PAD:h * HEAD_PAD + dh]
            kh = krow[:, h * HEAD_PAD:h * HEAD_PAD + dh]
            vh = vrow[:, h * HEAD_PAD:h * HEAD_PAD + dh]
            ig = gates[:, TAIL_IG + h:TAIL_IG + h + 1]
            lf = gates[:, TAIL_LF + h:TAIL_LF + h + 1]
            m_inter = lf + m0row[:, h:h + 1]
            mt = jnp.maximum(m_inter, ig)
            w = jnp.exp(ig - mt)
            g = jnp.exp(m_inter - mt)
            qk = jnp.sum(qh * kh, axis=1, keepdims=True) * w
            c0 = c0_ref[i, h]
            n0 = n0_ref[i, pl.ds(h, 1), :]
            qc = _dot(jnp.broadcast_to(qh, (8, dh)).astype(BF16), c0.astype(BF16))[0:1, :]
            qn = jnp.sum(qh * n0.astype(BF16).astype(F32), axis=1, keepdims=True)
            num = g * qc + qk * vh
            den = g * qn + qk
            hs.append(num / jnp.maximum(jnp.abs(den), jnp.exp(-mt)))
            kcol = _dot_nt(eye, jnp.broadcast_to(kh, (8, dh)).astype(BF16))[:, 0:1]
            c_ref[i, h] = g * c0 + (w * kcol) * vh
            n_ref[i, pl.ds(h, 1), :] = g * n0 + w * kh
            mrow = jnp.where(lane == h, mt, mrow)
        h_scr[pl.ds(i, 1), :] = jnp.concatenate(hs, axis=1)
        m_ref[pl.ds(i, 1), :] = mrow
        return 0

    lax.fori_loop(0, tb, body, 0)
    y_ref[...] = _rms(h_scr[...] * jax.nn.sigmoid(mo_ref[...]), gout_ref[...])


def _mlstm_decode(mq, mk, mv, mo, gcol, c0, n0, m0, wl, *, tb):
    n = mq.shape[0]
    nh = wl['m_heads']
    dh = mo.shape[1] // nh
    m0p = jnp.pad(m0, ((0, 0), (0, HEAD_PAD - nh)))
    tok = lambda w: pl.BlockSpec((tb, w), lambda i: (i, 0))
    kern = functools.partial(_mlstm_decode_kernel, n_heads=nh, tb=tb, dh=dh)
    y, c, nn, m = pl.pallas_call(
        kern,
        out_shape=[jax.ShapeDtypeStruct((n, nh * dh), F32), jax.ShapeDtypeStruct(c0.shape, F32),
                   jax.ShapeDtypeStruct(n0.shape, F32), jax.ShapeDtypeStruct((n, HEAD_PAD), F32)],
        grid=(n // tb,),
        in_specs=[tok(nh * HEAD_PAD), tok(nh * HEAD_PAD), tok(nh * HEAD_PAD), tok(nh * dh), tok(HEAD_PAD),
                  pl.BlockSpec((tb, nh, dh, dh), lambda i: (i, 0, 0, 0)),
                  pl.BlockSpec((tb, nh, dh), lambda i: (i, 0, 0)), tok(HEAD_PAD),
                  _const_spec(wl['g_mlstm'].shape)],
        out_specs=[tok(nh * dh), pl.BlockSpec((tb, nh, dh, dh), lambda i: (i, 0, 0, 0)),
                   pl.BlockSpec((tb, nh, dh), lambda i: (i, 0, 0)), tok(HEAD_PAD)],
        scratch_shapes=[pltpu.VMEM((tb, nh * HEAD_PAD), F32)] * 3 + [pltpu.VMEM((tb, nh * dh), F32)],
        compiler_params=_params(("parallel",)), name="mlstm_decode")(
        mq, mk, mv, mo, gcol, c0, n0, m0p, wl['g_mlstm'])
    return y, c, nn, m[:, :nh]


def _outffn_kernel(x_ref, ya_ref, yb_ref, yc_ref, gmla_ref, wout_ref, gpost_ref, gfpre_ref, wg_ref, wu_ref, wd_ref,
                   gfpost_ref, o_ref, *, ff_chunk):
    yb = _rms(yb_ref[...], gmla_ref[...])
    y = jnp.concatenate([ya_ref[...], yb, yc_ref[...]], axis=1).astype(BF16)
    x1 = x_ref[...] + _rms(_dot(y, wout_ref[...]), gpost_ref[...])
    h2 = _rms(x1, gfpre_ref[...]).astype(BF16)
    dff = wg_ref.shape[1]
    f = None
    for c0 in range(0, dff, ff_chunk):
        a = jax.nn.silu(_dot(h2, wg_ref[:, c0:c0 + ff_chunk])) * _dot(h2, wu_ref[:, c0:c0 + ff_chunk])
        part = _dot(a.astype(BF16), wd_ref[c0:c0 + ff_chunk, :])
        f = part if f is None else f + part
    o_ref[...] = x1 + _rms(f, gfpost_ref[...])


def _outffn(x, ya, yb, yc, wl, *, tm):
    n, d = x.shape
    weights = [wl['g_mla'], wl['wout'], wl['g_post'], wl['g_fpre'], wl['wg'], wl['wu'], wl['wd'], wl['g_fpost']]
    row = lambda w: pl.BlockSpec((tm, w), lambda i: (i, 0))
    dff = wl['wg'].shape[1]
    kern = functools.partial(_outffn_kernel, ff_chunk=dff // 2)
    return pl.pallas_call(
        kern, out_shape=jax.ShapeDtypeStruct((n, d), F32), grid=(n // tm,),
        in_specs=[row(d), row(ya.shape[1]), row(yb.shape[1]), row(yc.shape[1])]
        + [_const_spec(w.shape) for w in weights],
        out_specs=row(d), compiler_params=_params(("parallel",)), name="outffn")(x, ya, yb, yc, *weights)


def _rope_tables(positions):
    pos = np.asarray(positions, dtype=np.float64)[:, None]
    inv = ROPE_THETA ** (-np.arange(ROT, dtype=np.float64) / ROT)
    cos, sin = np.cos(pos * inv), np.sin(pos * inv)
    n = pos.shape[0]

    c = np.zeros((n, HEAD_PAD))
    lo = np.zeros((n, HEAD_PAD))
    hi = np.zeros((n, HEAD_PAD))
    c[:, 0:ROT] = cos
    c[:, ROT:2 * ROT] = cos
    lo[:, 0:ROT] = -sin
    hi[:, ROT:2 * ROT] = sin
    qc = np.zeros((HEAD_PAD, n))
    qs = np.zeros((HEAD_PAD, n))
    qc[:MLA_NOPE] = 1.0
    qc[MLA_NOPE:MLA_NOPE + ROT] = cos.T
    qc[MLA_NOPE + ROT:MLA_NOPE + 2 * ROT] = cos.T
    qs[MLA_NOPE:MLA_NOPE + ROT] = -sin.T
    qs[MLA_NOPE + ROT:MLA_NOPE + 2 * ROT] = sin.T
    return [jnp.asarray(a, dtype=F32) for a in (c, lo, hi, qc, qs)]


def _pad_heads(w, n_heads):
    rows = w.shape[0]
    w3 = w.reshape(rows, n_heads, -1)
    return jnp.pad(w3, ((0, 0), (0, 0), (0, HEAD_PAD - w3.shape[2]))).reshape(rows, n_heads * HEAD_PAD)


def _prep_layer(l, p):
    f = lambda name: p[name][l]
    row = lambda a: a.reshape(1, -1).astype(F32)
    w_in = f('w_in')
    d = w_in.shape[0]
    nh = p['mla_w_uk'].shape[2]
    mh = p['mlstm_b_i'].shape[1]
    o = 0
    cols = {}
    for name, width in (('u', 256), ('cq', 384), ('ckv', 256), ('kr', MLA_ROPE), ('mq', 256), ('mk', 256),
                        ('mv', 256), ('mo', 256), ('mi', mh), ('mf', mh)):
        cols[name] = w_in[:, o:o + width]
        o += width
    tail = jnp.concatenate([cols['kr'], cols['mi'], cols['mf'],
                            jnp.zeros((d, HEAD_PAD - MLA_ROPE - 2 * mh), F32)], axis=1)
    wz = jnp.concatenate([cols['u'], cols['cq'], cols['ckv'], _pad_heads(cols['mq'], mh), _pad_heads(cols['mk'], mh),
                          _pad_heads(cols['mv'], mh), cols['mo'], tail], axis=1).astype(BF16)
    bcol = jnp.zeros((1, HEAD_PAD), F32).at[0, TAIL_IG:TAIL_IG + mh].set(f('mlstm_b_i'))
    bcol = bcol.at[0, TAIL_LF:TAIL_LF + mh].set(f('mlstm_b_f'))
    brow = jnp.concatenate([f('mlstm_b_i'), f('mlstm_b_f')]).reshape(2 * mh, 1)
    one_lane = lambda n: jnp.zeros((n, HEAD_PAD), F32).at[:, MLA_NOPE].set(1.0).reshape(1, n * HEAD_PAD)

    w_uk = f('mla_w_uk')
    w_uv = f('mla_w_uv')
    r = w_uk.shape[0]
    wk_top = jnp.pad(w_uk, ((0, 0), (0, 0), (0, HEAD_PAD - MLA_NOPE))).reshape(r, nh * HEAD_PAD)
    place = jnp.zeros((HEAD_PAD, nh, HEAD_PAD), F32)
    e = jnp.arange(MLA_ROPE)
    place = place.at[e, :, MLA_NOPE + e].set(1.0)

    g_ = p['ssm_a_re'].shape[1]
    eye_g = jnp.eye(g_, dtype=F32)
    bre = jnp.einsum('gpc,gh->gchp', f('ssm_b_re'), eye_g).reshape(256, -1)
    bim = jnp.einsum('gpc,gh->gchp', f('ssm_b_im'), eye_g).reshape(256, -1)
    cre = jnp.einsum('gcp,gh->gphc', f('ssm_c_re'), eye_g).reshape(-1, 256)
    cim = jnp.einsum('gcp,gh->gphc', f('ssm_c_im'), eye_g).reshape(-1, 256)
    ns = bre.shape[1]
    return dict(
        n_heads=nh, m_heads=mh,
        g_pre=row(f('norm_mix_pre')), wz=wz,
        wgt=jnp.concatenate([cols['mi'], cols['mf']], axis=1).T.astype(BF16),
        wkt=cols['mk'].T.astype(BF16), bcol=bcol, brow=brow,
        qn=row(f('mla_q_norm')), wuqt=_pad_heads(f('mla_w_uq'), nh).T.astype(BF16), kvn=row(f('mla_kv_norm')),
        wk_top=wk_top.astype(BF16), wk_bot=place.reshape(HEAD_PAD, nh * HEAD_PAD).astype(BF16),
        wvt=jnp.pad(w_uv, ((0, 0), (0, 0), (0, HEAD_PAD - w_uv.shape[2]))).reshape(r, nh * HEAD_PAD).T.astype(BF16),
        vone=one_lane(nh).reshape(nh * HEAD_PAD, 1), vonem=one_lane(mh),
        wuk_h=w_uk.transpose(1, 0, 2).astype(BF16),
        wuv_h=w_uv.transpose(1, 0, 2).astype(BF16),
        ssm_are=row(f('ssm_a_re')), ssm_aim=row(f('ssm_a_im')),
        ssm_ldt=jnp.repeat(f('ssm_log_dt'), ns // g_).reshape(1, ns),
        ssm_bre=bre, ssm_bim=bim, ssm_ccat=jnp.concatenate([cre, -cim], axis=0).astype(BF16),
        ssm_d=row(f('ssm_d')), ssm_wglu=f('ssm_w_glu').astype(BF16), ssm_bglu=row(f('ssm_b_glu')),
        g_ssm=row(f('out_norm_ssm')), g_mla=row(f('out_norm_mla')), g_mlstm=row(f('out_norm_mlstm')),
        wout=f('w_out').astype(BF16), g_post=row(f('norm_mix_post')), g_fpre=row(f('norm_ffn_pre')),
        wg=f('ffn_w_gate').astype(BF16), wu=f('ffn_w_up').astype(BF16), wd=f('ffn_w_down').astype(BF16),
        g_fpost=row(f('norm_ffn_post')))


def kernel(x_prompt, x_sample, cache_ckv, cache_krope, page_table, state_ssm_re, state_ssm_im, state_mlstm_C, state_mlstm_n, state_mlstm_m, norm_mix_pre, norm_mix_post, norm_ffn_pre, norm_ffn_post, w_in, ssm_a_re, ssm_a_im, ssm_b_re, ssm_b_im, ssm_c_re, ssm_c_im, ssm_d, ssm_log_dt, ssm_w_glu, ssm_b_glu, mla_q_norm, mla_w_uq, mla_kv_norm, mla_w_uk, mla_w_uv, mlstm_b_i, mlstm_b_f, out_norm_ssm, out_norm_mla, out_norm_mlstm, w_out, ffn_w_gate, ffn_w_up, ffn_w_down):
    p = dict(norm_mix_pre=norm_mix_pre, norm_mix_post=norm_mix_post, norm_ffn_pre=norm_ffn_pre,
             norm_ffn_post=norm_ffn_post, w_in=w_in, ssm_a_re=ssm_a_re, ssm_a_im=ssm_a_im, ssm_b_re=ssm_b_re,
             ssm_b_im=ssm_b_im, ssm_c_re=ssm_c_re, ssm_c_im=ssm_c_im, ssm_d=ssm_d, ssm_log_dt=ssm_log_dt,
             ssm_w_glu=ssm_w_glu, ssm_b_glu=ssm_b_glu, mla_q_norm=mla_q_norm, mla_w_uq=mla_w_uq,
             mla_kv_norm=mla_kv_norm, mla_w_uk=mla_w_uk, mla_w_uv=mla_w_uv, mlstm_b_i=mlstm_b_i,
             mlstm_b_f=mlstm_b_f, out_norm_ssm=out_norm_ssm, out_norm_mla=out_norm_mla,
             out_norm_mlstm=out_norm_mlstm, w_out=w_out, ffn_w_gate=ffn_w_gate, ffn_w_up=ffn_w_up,
             ffn_w_down=ffn_w_down)
    depth = w_in.shape[0]
    bsz, seq, d = x_prompt.shape
    db, dec_t, _ = x_sample.shape
    assert dec_t == 1, "the sample group advances one token per sequence"
    n_pages, page = page_table.shape[1], cache_ckv.shape[2]
    past = n_pages * page
    g_, ns_g = ssm_a_re.shape[1], ssm_a_re.shape[2]
    ns = g_ * ns_g

    tm_p = min(TM_PROJ, seq)
    tm_f = min(TM_FFN, seq)
    tc = min(T_SSM, seq)
    tq = min(T_ATT, seq)
    tl = min(T_MLSTM, seq)
    tb = min(TOK_MLSTM_DEC, db)
    pps = min(PAGES_PER_STEP, n_pages)
    assert seq % tm_p == 0 and seq % tm_f == 0 and seq % tc == 0 and seq % tq == 0 and seq % tl == 0
    assert db % tb == 0 and n_pages % pps == 0

    tabs_p = _rope_tables(np.arange(seq))
    tabs_s = _rope_tables(np.full((db,), past))
    pt_flat = page_table.reshape(-1).astype(jnp.int32)
    cache_krt = jnp.swapaxes(cache_krope, 2, 3)

    xp = x_prompt.reshape(bsz * seq, d)
    xs = x_sample.reshape(db, d)
    outs_p, outs_s = [], []
    for l in range(depth):
        wl = _prep_layer(l, p)
        nh, mh = wl['n_heads'], wl['m_heads']
        dv = wl['wuv_h'].shape[2]

        (u, qT, ckvn, kr, kpad, vT, mq, mk, mv, mo, gcol, grow, kT) = _inproj(
            xp, wl, tabs_p, tm=tm_p, tk=tq, tiles_per_seq=seq // tm_p)
        ya, s_re, s_im = _ssm_prompt(u, wl, batch=bsz, seq=seq, tc=tc)
        yb = _attn_prompt(qT, kpad, vT, batch=bsz, seq=seq, n_heads=nh, tq=tq, dv=dv)
        yc, c_p, n_p, m_p = _mlstm_prompt(mq, mk, kT, mv, mo, gcol, grow, wl, batch=bsz, seq=seq, tl=tl)
        xp = _outffn(xp, ya, yb, yc, wl, tm=tm_f)
        outs_p.append((ckvn.reshape(bsz, seq, -1), kr.reshape(bsz, seq, -1),
                       s_re.reshape(bsz, g_, ns_g), s_im.reshape(bsz, g_, ns_g), c_p, n_p, m_p))

        (u, qT, ckvn, kr, _, _, mq, mk, mv, mo, gcol, _, _) = _inproj(
            xs, wl, tabs_s, tm=db, tk=db, tiles_per_seq=1)
        ya, h_re, h_im = _ssm_decode(u, state_ssm_re[l].reshape(db, ns), state_ssm_im[l].reshape(db, ns), wl)
        qlat, qrope = _qlat(qT, wl)
        olat = _dattn(qlat.transpose(2, 0, 1), qrope.transpose(2, 0, 1), ckvn.reshape(db, 1, -1),
                      kr.reshape(db, 1, -1), cache_ckv, cache_krt, pt_flat, layer=l, n_pages=n_pages, pps=pps)
        yb = _uv(olat.transpose(1, 0, 2), wl)
        yc, c_s, n_s, m_s = _mlstm_decode(mq, mk, mv, mo, gcol, state_mlstm_C[l], state_mlstm_n[l],
                                          state_mlstm_m[l], wl, tb=tb)
        xs = _outffn(xs, ya, yb, yc, wl, tm=db)
        outs_s.append((ckvn.reshape(db, 1, -1), kr.reshape(db, 1, -1),
                       h_re.reshape(db, g_, ns_g), h_im.reshape(db, g_, ns_g), c_s, n_s, m_s))

    stk = lambda lst, i: jnp.stack([s[i] for s in lst], axis=0)
    return (xp.reshape(bsz, seq, d), xs.reshape(db, 1, d),
            stk(outs_p, 0), stk(outs_p, 1), stk(outs_s, 0), stk(outs_s, 1),
            stk(outs_p, 2), stk(outs_p, 3), stk(outs_s, 2), stk(outs_s, 3),
            stk(outs_p, 4), stk(outs_p, 5), stk(outs_p, 6),
            stk(outs_s, 4), stk(outs_s, 5), stk(outs_s, 6))
```
